```python
import math
import jax, jax.numpy as jnp
from jax import lax
import numpy as np

D_MODEL = 1024
BATCH = 4
SEQ = 4096
DEPTH = 4
DEC_BATCH = 32
DEC_SEQ = 1
PAST_LEN = 8192
PAGE_SIZE = 128

MIX_WIDTH = D_MODEL
ATTN_WIDTH = MIX_WIDTH // 2
CONV_WIDTH = MIX_WIDTH - ATTN_WIDTH
N_DIFF_HEADS = 4
DIFF_HEAD_DIM = ATTN_WIDTH // (2 * N_DIFF_HEADS)
ROT_DIM = DIFF_HEAD_DIM // 4
ROPE_THETA = 500000.0
CONV_K = 3
N_MEM = 256
N_MEM_HEADS = 4
MEM_HEAD_DIM = D_MODEL // N_MEM_HEADS
D_FF = 11 * D_MODEL // 4
FFN_K = 3
Q_BLOCK = 128
EPS = 1e-6
NEG_INF = -1e30
SPLITS = [ATTN_WIDTH, 2 * ATTN_WIDTH, 3 * ATTN_WIDTH,
          3 * ATTN_WIDTH + CONV_WIDTH, 3 * ATTN_WIDTH + 2 * CONV_WIDTH]
IN_COLS = 3 * ATTN_WIDTH + 3 * CONV_WIDTH

kernel_name = "hymba_diffattn_shortconv_decoder_step"


def rmsnorm(x, g):
    xf = x.astype(jnp.float32)
    y = xf * lax.rsqrt(jnp.mean(xf * xf, axis=-1, keepdims=True) + EPS)
    return (y * g.astype(jnp.float32)).astype(x.dtype)


def rope_partial(x, pos):
    half = ROT_DIM // 2
    inv = jnp.power(ROPE_THETA, -jnp.arange(half, dtype=jnp.float32) * 2.0 / ROT_DIM)
    ang = pos.astype(jnp.float32)[:, None] * inv[None, :]
    cos = jnp.cos(ang)[:, None, None, :]
    sin = jnp.sin(ang)[:, None, None, :]
    xr = x[..., :ROT_DIM].astype(jnp.float32)
    x1, x2 = xr[..., :half], xr[..., half:]
    rot = jnp.concatenate([x1 * cos - x2 * sin, x2 * cos + x1 * sin], axis=-1).astype(x.dtype)
    return jnp.concatenate([rot, x[..., ROT_DIM:]], axis=-1)


def diff_attend(q, k, v, q_pos, k_pos, lam):
    s = jnp.einsum('bqhcd,bkhcd->bhcqk', q, k).astype(jnp.float32) * (DIFF_HEAD_DIM ** -0.5)
    s = jnp.where(k_pos[None, :] <= q_pos[:, None], s, NEG_INF)
    p = jax.nn.softmax(s, axis=-1)
    p = p[:, :, 0] - lam * p[:, :, 1]
    return jnp.einsum('bhqk,bkhe->bqhe', p.astype(v.dtype), v)


def causal_dwconv(u, hist, w):
    kw = w.shape[0]
    t = u.shape[1]
    cat = jnp.concatenate([hist, u], axis=1)
    y = sum(w[j] * cat[:, j:j + t] for j in range(kw))
    return y, cat[:, -(kw - 1):]


def memory_kv(mem, g, w_ck, w_cv):
    b = mem.shape[0]
    hm = rmsnorm(mem, g)
    mk = (hm @ w_ck).reshape(b, N_MEM, N_MEM_HEADS, MEM_HEAD_DIM)
    mv = (hm @ w_cv).reshape(b, N_MEM, N_MEM_HEADS, MEM_HEAD_DIM)
    return mk, mv


def cross_attend(h, mem_k, mem_v, w_cq, w_co):
    b, t, _ = h.shape
    q = (h @ w_cq).reshape(b, t, N_MEM_HEADS, MEM_HEAD_DIM)
    s = jnp.einsum('bqhd,bmhd->bhqm', q, mem_k).astype(jnp.float32) * (MEM_HEAD_DIM ** -0.5)
    p = jax.nn.softmax(s, axis=-1)
    o = jnp.einsum('bhqm,bmhd->bqhd', p.astype(mem_v.dtype), mem_v).reshape(b, t, D_MODEL)
    return o @ w_co


def layer_forward(x, pos, past_k, past_v, conv_hist, ffn_hist, mem_k, mem_v, lp, lam_init, blocked):
    b, t, _ = x.shape
    h = rmsnorm(x, lp['norm_mix'])
    proj = h @ lp['w_in']
    q, k, v, gate_b, gate_c, u = jnp.split(proj, SPLITS, axis=-1)
    q = rope_partial(q.reshape(b, t, N_DIFF_HEADS, 2, DIFF_HEAD_DIM), pos)
    k = rope_partial(k.reshape(b, t, N_DIFF_HEADS, 2, DIFF_HEAD_DIM), pos)
    v = v.reshape(b, t, N_DIFF_HEADS, 2 * DIFF_HEAD_DIM)
    if past_k is None:
        k_all, v_all = k, v
    else:
        k_all = jnp.concatenate([past_k, k], axis=1)
        v_all = jnp.concatenate([past_v, v], axis=1)
    k_pos = jnp.arange(k_all.shape[1])
    f32 = jnp.float32
    lam = (jnp.exp(jnp.sum(lp['lq1'].astype(f32) * lp['lk1'].astype(f32)))
           - jnp.exp(jnp.sum(lp['lq2'].astype(f32) * lp['lk2'].astype(f32))) + lam_init)
    if blocked:
        nb = t // Q_BLOCK
        qb = jnp.moveaxis(q.reshape(b, nb, Q_BLOCK, N_DIFF_HEADS, 2, DIFF_HEAD_DIM), 1, 0)
        pb = pos.reshape(nb, Q_BLOCK)
        o = lax.map(lambda qp: diff_attend(qp[0], k_all, v_all, qp[1], k_pos, lam), (qb, pb))
        o = jnp.moveaxis(o, 0, 1).reshape(b, t, N_DIFF_HEADS, 2 * DIFF_HEAD_DIM)
    else:
        o = diff_attend(q, k_all, v_all, pos, k_pos, lam)
    o = rmsnorm(o, lp['subln_g']) * (1.0 - lam_init)
    attn_out = o.reshape(b, t, ATTN_WIDTH)
    cu, new_conv_hist = causal_dwconv(gate_c * u, conv_hist, lp['conv_w'])
    conv_out = gate_b * cu
    x = x + jnp.concatenate([attn_out, conv_out], axis=-1) @ lp['w_out']
    h = rmsnorm(x, lp['norm_cross'])
    x = x + cross_attend(h, mem_k, mem_v, lp['w_cq'], lp['w_co'])
    h = rmsnorm(x, lp['norm_ffn'])
    up, new_ffn_hist = causal_dwconv(h @ lp['w_up'], ffn_hist, lp['ffn_conv_w'])
    a, g = jnp.split(up, 2, axis=-1)
    x = x + (jax.nn.silu(g) * a) @ lp['w_down']
    return x, k, v, new_conv_hist, new_ffn_hist


def setup_inputs(seed: int = 0) -> dict:
    key = jax.random.key(seed)
    ks = jax.random.split(key, 40)
    f32 = jnp.float32
    n_pages = PAST_LEN // PAGE_SIZE
    n_used = DEC_BATCH * n_pages
    n_phys = n_used + max(1, n_used // 4)
    nrm = lambda i, shape, s: jax.random.normal(ks[i], shape, f32) * s
    gain = lambda i, shape: 1.0 + 0.02 * jax.random.normal(ks[i], shape, f32)
    page_table = jax.random.permutation(ks[0], n_phys)[:n_used].reshape(DEC_BATCH, n_pages).astype(jnp.int32)
    return {
        'x_prompt': nrm(1, (BATCH, SEQ, D_MODEL), 1.0),
        'x_sample': nrm(2, (DEC_BATCH, DEC_SEQ, D_MODEL), 1.0),
        'mem_prompt': nrm(3, (BATCH, N_MEM, D_MODEL), 1.0),
        'cache_k': nrm(4, (DEPTH, n_phys, PAGE_SIZE, N_DIFF_HEADS, 2, DIFF_HEAD_DIM), 1.0),
        'cache_v': nrm(5, (DEPTH, n_phys, PAGE_SIZE, N_DIFF_HEADS, 2 * DIFF_HEAD_DIM), 1.0),
        'state_conv': nrm(6, (DEPTH, DEC_BATCH, CONV_K - 1, CONV_WIDTH), 1.0),
        'state_ffn_conv': nrm(7, (DEPTH, DEC_BATCH, FFN_K - 1, 2 * D_FF), 1.0),
        'cache_mem_k': nrm(8, (DEPTH, DEC_BATCH, N_MEM, N_MEM_HEADS, MEM_HEAD_DIM), 1.0),
        'cache_mem_v': nrm(9, (DEPTH, DEC_BATCH, N_MEM, N_MEM_HEADS, MEM_HEAD_DIM), 1.0),
        'page_table': page_table,
        'w_in': nrm(10, (DEPTH, D_MODEL, IN_COLS), D_MODEL ** -0.5),
        'w_out': nrm(11, (DEPTH, MIX_WIDTH, D_MODEL), MIX_WIDTH ** -0.5),
        'conv_w': nrm(12, (DEPTH, CONV_K, CONV_WIDTH), CONV_K ** -0.5),
        'lambda_q1': nrm(13, (DEPTH, DIFF_HEAD_DIM), 0.1),
        'lambda_k1': nrm(14, (DEPTH, DIFF_HEAD_DIM), 0.1),
        'lambda_q2': nrm(15, (DEPTH, DIFF_HEAD_DIM), 0.1),
        'lambda_k2': nrm(16, (DEPTH, DIFF_HEAD_DIM), 0.1),
        'subln_g': gain(17, (DEPTH, 2 * DIFF_HEAD_DIM)),
        'norm_mix': gain(18, (DEPTH, D_MODEL)),
        'norm_cross': gain(19, (DEPTH, D_MODEL)),
        'norm_mem': gain(20, (DEPTH, D_MODEL)),
        'w_cq': nrm(21, (DEPTH, D_MODEL, D_MODEL), D_MODEL ** -0.5),
        'w_ck': nrm(22, (DEPTH, D_MODEL, D_MODEL), D_MODEL ** -0.5),
        'w_cv': nrm(23, (DEPTH, D_MODEL, D_MODEL), D_MODEL ** -0.5),
        'w_co': nrm(24, (DEPTH, D_MODEL, D_MODEL), D_MODEL ** -0.5),
        'norm_ffn': gain(25, (DEPTH, D_MODEL)),
        'w_up': nrm(26, (DEPTH, D_MODEL, 2 * D_FF), D_MODEL ** -0.5),
        'ffn_conv_w': nrm(27, (DEPTH, FFN_K, 2 * D_FF), FFN_K ** -0.5),
        'w_down': nrm(28, (DEPTH, D_FF, D_MODEL), D_FF ** -0.5),
        'norm_final': gain(29, (D_MODEL,)),
    }


def reference(x_prompt, x_sample, mem_prompt, cache_k, cache_v, state_conv, state_ffn_conv,
              cache_mem_k, cache_mem_v, page_table, w_in, w_out, conv_w, lambda_q1, lambda_k1,
              lambda_q2, lambda_k2, subln_g, norm_mix, norm_cross, norm_mem, w_cq, w_ck, w_cv,
              w_co, norm_ffn, w_up, ffn_conv_w, w_down, norm_final):
    bp, tp, _ = x_prompt.shape
    bs, ts, _ = x_sample.shape
    n_pages = page_table.shape[1]
    past_len = n_pages * cache_k.shape[2]
    pos_p = jnp.arange(tp)
    pos_s = past_len + jnp.arange(ts)
    xp, xs = x_prompt, x_sample
    kp_l, vp_l, cp_l, fp_l, mkp_l, mvp_l = [], [], [], [], [], []
    ks_l, vs_l, cs_l, fs_l = [], [], [], []
    for l in range(DEPTH):
        lam_init = 0.8 - 0.6 * math.exp(-0.3 * l)
        lp = {'norm_mix': norm_mix[l], 'w_in': w_in[l], 'w_out': w_out[l], 'conv_w': conv_w[l],
              'lq1': lambda_q1[l], 'lk1': lambda_k1[l], 'lq2': lambda_q2[l], 'lk2': lambda_k2[l],
              'subln_g': subln_g[l], 'norm_cross': norm_cross[l], 'w_cq': w_cq[l], 'w_co': w_co[l],
              'norm_ffn': norm_ffn[l], 'w_up': w_up[l], 'ffn_conv_w': ffn_conv_w[l], 'w_down': w_down[l]}
        mk_p, mv_p = memory_kv(mem_prompt, norm_mem[l], w_ck[l], w_cv[l])
        zc = jnp.zeros((bp, CONV_K - 1, CONV_WIDTH), xp.dtype)
        zf = jnp.zeros((bp, FFN_K - 1, 2 * D_FF), xp.dtype)
        xp, k_p, v_p, c_p, f_p = layer_forward(xp, pos_p, None, None, zc, zf, mk_p, mv_p, lp, lam_init, True)
        kp_l.append(k_p); vp_l.append(v_p); cp_l.append(c_p); fp_l.append(f_p)
        mkp_l.append(mk_p); mvp_l.append(mv_p)
        past_k = cache_k[l][page_table].reshape(bs, past_len, N_DIFF_HEADS, 2, DIFF_HEAD_DIM)
        past_v = cache_v[l][page_table].reshape(bs, past_len, N_DIFF_HEADS, 2 * DIFF_HEAD_DIM)
        xs, k_s, v_s, c_s, f_s = layer_forward(xs, pos_s, past_k, past_v, state_conv[l], state_ffn_conv[l],
                                               cache_mem_k[l], cache_mem_v[l], lp, lam_init, False)
        ks_l.append(k_s); vs_l.append(v_s); cs_l.append(c_s); fs_l.append(f_s)
    y_prompt = rmsnorm(xp, norm_final)
    y_sample = rmsnorm(xs, norm_final)
    return (y_prompt, y_sample,
            jnp.stack(kp_l), jnp.stack(vp_l), jnp.stack(cp_l), jnp.stack(fp_l),
            jnp.stack(mkp_l), jnp.stack(mvp_l),
            jnp.stack(ks_l), jnp.stack(vs_l), jnp.stack(cs_l), jnp.stack(fs_l))
```

```python
import functools
import math

import jax
import jax.numpy as jnp
from jax import lax
from jax.experimental import pallas as pl
from jax.experimental.pallas import tpu as pltpu

F32 = jnp.float32
BF16 = jnp.bfloat16

D_MODEL = 1024
ATTN_WIDTH = 512
CONV_WIDTH = 512
N_DIFF_HEADS = 4
DIFF_HEAD_DIM = 64
V_HEAD_DIM = 2 * DIFF_HEAD_DIM
ROT_DIM = DIFF_HEAD_DIM // 4
ROPE_THETA = 500000.0
N_MEM_HEADS = 4
MEM_HEAD_DIM = D_MODEL // N_MEM_HEADS
D_FF = 11 * D_MODEL // 4
EPS = 1e-6
NEG_INF = -1e30

LANES = 128
SUBLANES = 8
VMEM_LIMIT_BYTES = 56 * 1024 * 1024

TM_MIX = 512
TQ = 512
TM_FFN = 512
FF_CHUNK = 256
N_FF_CHUNKS = D_FF // FF_CHUNK
PAGES_PER_CHUNK = 16


def _dot(a, b):
    return jnp.dot(a, b, preferred_element_type=F32)


def _dot_nt(a, b):
    return lax.dot_general(a, b, (((1,), (1,)), ((), ())), preferred_element_type=F32)


def _rms(x, g):
    ms = jnp.mean(x * x, axis=-1, keepdims=True)
    return x * lax.rsqrt(ms + EPS) * g


def _rope(y, cos, sa, sb):
    outs = []
    for c in range(y.shape[1] // LANES):
        yc = y[:, c * LANES:(c + 1) * LANES]
        outs.append(yc * cos + pltpu.roll(yc, ROT_DIM // 2, 1) * sa
                    + pltpu.roll(yc, LANES - ROT_DIM // 2, 1) * sb)
    return jnp.concatenate(outs, axis=1)


def _rope_tables(pos):
    half = ROT_DIM // 2
    inv = jnp.power(ROPE_THETA, -jnp.arange(half, dtype=F32) * 2.0 / ROT_DIM)
    ang = pos.astype(F32)[:, None] * inv[None, :]
    cos, sin = jnp.cos(ang), jnp.sin(ang)
    n = pos.shape[0]
    ones = jnp.ones((n, DIFF_HEAD_DIM - ROT_DIM), F32)
    zeros = jnp.zeros_like(ones)
    z8 = jnp.zeros((n, half), F32)
    c64 = jnp.concatenate([cos, cos, ones], axis=1)
    sa64 = jnp.concatenate([z8, sin, zeros], axis=1)
    sb64 = jnp.concatenate([-sin, z8, zeros], axis=1)
    tile2 = lambda a: jnp.concatenate([a, a], axis=1)
    return tile2(c64), tile2(sa64), tile2(sb64)


def _lam(lq1, lk1, lq2, lk2, lam_init):
    return (jnp.exp(jnp.sum(lq1 * lk1, axis=-1, keepdims=True))
            - jnp.exp(jnp.sum(lq2 * lk2, axis=-1, keepdims=True)) + lam_init)


def _params(sem=None):
    return pltpu.CompilerParams(dimension_semantics=sem, vmem_limit_bytes=VMEM_LIMIT_BYTES)


def _memkv_kernel(mem_ref, g_ref, wk_ref, wv_ref, mk_ref, mv_ref, mkb_ref, mvb_ref):
    hm = _rms(mem_ref[...], g_ref[...]).astype(BF16)
    mk = _dot(hm, wk_ref[...])
    mv = _dot(hm, wv_ref[...])
    mk_ref[...] = mk
    mv_ref[...] = mv
    mkb_ref[...] = mk.astype(BF16)
    mvb_ref[...] = mv.astype(BF16)


def _memkv(mem, g3, wk, wv):
    depth = wk.shape[0]
    b, n, d = mem.shape
    wspec = pl.BlockSpec((None, d, d), lambda l, i: (l, 0, 0))
    ospec = pl.BlockSpec((None, None, n, d), lambda l, i: (l, i, 0, 0))
    return pl.pallas_call(
        _memkv_kernel,
        grid=(depth, b),
        in_specs=[pl.BlockSpec((None, n, d), lambda l, i: (i, 0, 0)),
                  pl.BlockSpec((None, 1, d), lambda l, i: (l, 0, 0)),
                  wspec, wspec],
        out_specs=[ospec, ospec, ospec, ospec],
        out_shape=[jax.ShapeDtypeStruct((depth, b, n, d), F32)] * 2
        + [jax.ShapeDtypeStruct((depth, b, n, d), BF16)] * 2,
        compiler_params=_params(("arbitrary", "arbitrary")),
        name="memkv",
    )(mem, g3, wk, wv)


def _mix_in_kernel(x_ref, g_ref, w_ref, cw_ref, cos_ref, sa_ref, sb_ref,
                   q1_ref, q2_ref, k_ref, v_ref, kb_ref, vb_ref, co_ref, hist_ref,
                   cbuf, *, tm):
    t = pl.program_id(1)
    h = _rms(x_ref[...], g_ref[...]).astype(BF16)
    cos, sa, sb = cos_ref[...], sa_ref[...], sb_ref[...]

    def proj(i):
        return _dot(h, w_ref[:, i * ATTN_WIDTH:(i + 1) * ATTN_WIDTH])

    lane = lax.broadcasted_iota(jnp.int32, (tm, ATTN_WIDTH), 1)
    first_map = (lane % V_HEAD_DIM) < DIFF_HEAD_DIM
    q = _rope(proj(0), cos, sa, sb) * (DIFF_HEAD_DIM ** -0.5)
    q1_ref[...] = jnp.where(first_map, q, 0.0).astype(BF16)
    q2_ref[...] = jnp.where(first_map, 0.0, q).astype(BF16)
    k = _rope(proj(1), cos, sa, sb)
    k_ref[...] = k
    kb_ref[...] = k.astype(BF16)
    v = proj(2)
    v_ref[...] = v
    vb_ref[...] = v.astype(BF16)

    gate_b = proj(3)
    cu = proj(4) * proj(5)

    @pl.when(t == 0)
    def _():
        cbuf[0:SUBLANES, :] = jnp.zeros((SUBLANES, CONV_WIDTH), F32)

    cbuf[SUBLANES:SUBLANES + tm, :] = cu
    cw = cw_ref[...]
    y = (cw[0:1] * cbuf[SUBLANES - 2:SUBLANES - 2 + tm, :]
         + cw[1:2] * cbuf[SUBLANES - 1:SUBLANES - 1 + tm, :]
         + cw[2:3] * cu)
    co_ref[...] = (gate_b * y).astype(BF16)
    tail = cu[tm - SUBLANES:, :]
    hist_ref[...] = tail
    cbuf[0:SUBLANES, :] = tail


def _mix_in(x, norm3, w_in, conv_w, tables, layer):
    b, t, d = x.shape
    tm = TM_MIX
    cos, sa, sb = tables
    tok = lambda w, dt: jax.ShapeDtypeStruct((b, t, w), dt)
    tspec = pl.BlockSpec((tm, LANES), lambda i, j: (j, 0))
    ospec = pl.BlockSpec((None, tm, ATTN_WIDTH), lambda i, j: (i, j, 0))
    return pl.pallas_call(
        functools.partial(_mix_in_kernel, tm=tm),
        grid=(b, t // tm),
        in_specs=[pl.BlockSpec((None, tm, d), lambda i, j: (i, j, 0)),
                  pl.BlockSpec((None, 1, d), lambda i, j: (layer, 0, 0)),
                  pl.BlockSpec((None, d, w_in.shape[2]), lambda i, j: (layer, 0, 0)),
                  pl.BlockSpec((None, 3, CONV_WIDTH), lambda i, j: (layer, 0, 0)),
                  tspec, tspec, tspec],
        out_specs=[ospec] * 7 + [pl.BlockSpec((None, SUBLANES, CONV_WIDTH), lambda i, j: (i, 0, 0))],
        out_shape=[tok(ATTN_WIDTH, BF16), tok(ATTN_WIDTH, BF16), tok(ATTN_WIDTH, F32), tok(ATTN_WIDTH, F32),
                   tok(ATTN_WIDTH, BF16), tok(ATTN_WIDTH, BF16), tok(CONV_WIDTH, BF16),
                   jax.ShapeDtypeStruct((b, SUBLANES, CONV_WIDTH), F32)],
        scratch_shapes=[pltpu.VMEM((tm + SUBLANES, CONV_WIDTH), F32)],
        compiler_params=_params(("arbitrary", "arbitrary")),
        name="mix_in",
    )(x, norm3, w_in, conv_w, cos, sa, sb)


def _attn_kernel(q1_ref, q2_ref, k_ref, v_ref, lq1_ref, lk1_ref, lq2_ref, lk2_ref, sg_ref,
                 o_ref, m_sc, l_sc, acc_sc, *, tq, lam_init):
    i = pl.program_id(1)
    j = pl.program_id(2)

    @pl.when(j == 0)
    def _():
        m_sc[...] = jnp.full(m_sc.shape, NEG_INF, F32)
        l_sc[...] = jnp.zeros(l_sc.shape, F32)
        acc_sc[...] = jnp.zeros(acc_sc.shape, F32)

    def step(masked):
        if masked:
            row = lax.broadcasted_iota(jnp.int32, (tq, tq), 0)
            col = lax.broadcasted_iota(jnp.int32, (tq, tq), 1)
            keep = col <= row
        for h in range(N_DIFF_HEADS):
            hs = slice(h * V_HEAD_DIM, (h + 1) * V_HEAD_DIM)
            kh = k_ref[:, hs]
            vh = v_ref[:, hs]
            for c, q_ref in enumerate((q1_ref, q2_ref)):
                idx = 2 * h + c
                s = _dot_nt(q_ref[:, hs], kh)
                if masked:
                    s = jnp.where(keep, s, NEG_INF)
                m_old = m_sc[idx]
                m_new = jnp.maximum(m_old, jnp.max(s, axis=1, keepdims=True))
                alpha = jnp.exp(m_old - m_new)
                p = jnp.exp(s - m_new)
                l_sc[idx] = alpha * l_sc[idx] + jnp.sum(p, axis=1, keepdims=True)
                acc_sc[idx] = alpha * acc_sc[idx] + _dot(p.astype(BF16), vh)
                m_sc[idx] = m_new

    @pl.when(j < i)
    def _():
        step(False)

    @pl.when(j == i)
    def _():
        step(True)
        lam = _lam(lq1_ref[...], lk1_ref[...], lq2_ref[...], lk2_ref[...], lam_init)
        sg = sg_ref[...]
        for h in range(N_DIFF_HEADS):
            o = acc_sc[2 * h] / l_sc[2 * h] - lam * (acc_sc[2 * h + 1] / l_sc[2 * h + 1])
            o = _rms(o, sg) * (1.0 - lam_init)
            o_ref[:, h * V_HEAD_DIM:(h + 1) * V_HEAD_DIM] = o.astype(BF16)


def _attn(q1, q2, kb, vb, lams, sg3, layer, lam_init):
    b, t, w = q1.shape
    tq = TQ
    nq = t // tq
    qspec = pl.BlockSpec((None, tq, w), lambda bi, i, j: (bi, i, 0))
    kspec = pl.BlockSpec((None, tq, w), lambda bi, i, j: (bi, jnp.minimum(i, j), 0))
    lspec = pl.BlockSpec((None, 1, DIFF_HEAD_DIM), lambda bi, i, j: (layer, 0, 0))
    return pl.pallas_call(
        functools.partial(_attn_kernel, tq=tq, lam_init=lam_init),
        grid=(b, nq, nq),
        in_specs=[qspec, qspec, kspec, kspec, lspec, lspec, lspec, lspec,
                  pl.BlockSpec((None, 1, V_HEAD_DIM), lambda bi, i, j: (layer, 0, 0))],
        out_specs=pl.BlockSpec((None, tq, w), lambda bi, i, j: (bi, i, 0)),
        out_shape=jax.ShapeDtypeStruct((b, t, w), BF16),
        scratch_shapes=[pltpu.VMEM((2 * N_DIFF_HEADS, tq, 1), F32),
                        pltpu.VMEM((2 * N_DIFF_HEADS, tq, 1), F32),
                        pltpu.VMEM((2 * N_DIFF_HEADS, tq, V_HEAD_DIM), F32)],
        compiler_params=_params(("arbitrary", "arbitrary", "arbitrary")),
        name="diff_attn",
    )(q1, q2, kb, vb, *lams, sg3)


def _cross_attend(qc, mk_ref, mv_ref):
    outs = []
    for h in range(N_MEM_HEADS):
        hs = slice(h * MEM_HEAD_DIM, (h + 1) * MEM_HEAD_DIM)
        s = _dot_nt(qc[:, hs], mk_ref[:, hs])
        m = jnp.max(s, axis=1, keepdims=True)
        p = jnp.exp(s - m)
        l = jnp.sum(p, axis=1, keepdims=True)
        outs.append((_dot(p.astype(BF16), mv_ref[:, hs]) / l).astype(BF16))
    return jnp.concatenate(outs, axis=1)


def _mix_out_kernel(x_ref, at_ref, co_ref, wo_ref, g_ref, wq_ref, mk_ref, mv_ref, wco_ref, o_ref):
    x1 = (x_ref[...] + _dot(at_ref[...], wo_ref[0:ATTN_WIDTH, :])
          + _dot(co_ref[...], wo_ref[ATTN_WIDTH:ATTN_WIDTH + CONV_WIDTH, :]))
    hq = _rms(x1, g_ref[...]).astype(BF16)
    qc = (_dot(hq, wq_ref[...]) * (MEM_HEAD_DIM ** -0.5)).astype(BF16)
    o = _cross_attend(qc, mk_ref, mv_ref)
    o_ref[...] = x1 + _dot(o, wco_ref[...])


def _mix_out(x, at, co, w_out, norm3, w_cq, mkb, mvb, w_co, layer):
    b, t, d = x.shape
    tm = TM_MIX
    n_mem = mkb.shape[2]
    wspec = pl.BlockSpec((None, d, d), lambda i, j: (layer, 0, 0))
    hspec = pl.BlockSpec((None, tm, ATTN_WIDTH), lambda i, j: (i, j, 0))
    mspec = pl.BlockSpec((None, None, n_mem, d), lambda i, j: (layer, i, 0, 0))
    xspec = pl.BlockSpec((None, tm, d), lambda i, j: (i, j, 0))
    return pl.pallas_call(
        _mix_out_kernel,
        grid=(b, t // tm),
        in_specs=[xspec, hspec, hspec, wspec,
                  pl.BlockSpec((None, 1, d), lambda i, j: (layer, 0, 0)),
                  wspec, mspec, mspec, wspec],
        out_specs=xspec,
        out_shape=jax.ShapeDtypeStruct((b, t, d), F32),
        compiler_params=_params(("arbitrary", "arbitrary")),
        name="mix_out",
    )(x, at, co, w_out, norm3, w_cq, mkb, mvb, w_co)


def _ffn_kernel(x_ref, g_ref, wa_ref, wg_ref, cwa_ref, cwg_ref, wd_ref, gf_ref,
                o_ref, hist_ref, hb_sc, acc_sc, buf_a, buf_g, car_a, car_g, *, tm, final):
    t = pl.program_id(1)
    x = x_ref[...]
    hb_sc[...] = _rms(x, g_ref[...]).astype(BF16)
    acc_sc[...] = x

    @pl.when(t == 0)
    def _():
        car_a[...] = jnp.zeros(car_a.shape, F32)
        car_g[...] = jnp.zeros(car_g.shape, F32)

    def conv(u, buf, car, cw, c):
        buf[0:SUBLANES, :] = car[c]
        buf[SUBLANES:SUBLANES + tm, :] = u
        y = (cw[0:1] * buf[SUBLANES - 2:SUBLANES - 2 + tm, :]
             + cw[1:2] * buf[SUBLANES - 1:SUBLANES - 1 + tm, :]
             + cw[2:3] * u)
        car[c] = u[tm - SUBLANES:, :]
        return y

    def body(c, carry):
        hb = hb_sc[...]
        a = conv(_dot(hb, wa_ref[c]), buf_a, car_a, cwa_ref[c], c)
        g = conv(_dot(hb, wg_ref[c]), buf_g, car_g, cwg_ref[c], c)
        act = (jax.nn.silu(g) * a).astype(BF16)
        acc_sc[...] += _dot(act, wd_ref[c])
        return carry

    lax.fori_loop(0, N_FF_CHUNKS, body, 0)
    hist_ref[:, 0] = car_a[...]
    hist_ref[:, 1] = car_g[...]
    out = acc_sc[...]
    if final:
        out = _rms(out, gf_ref[...])
    o_ref[...] = out


def _ffn(x, norm3, wa, wg, cwa, cwg, wd, gf2, layer, final):
    b, t, d = x.shape
    tm = TM_FFN
    nch = N_FF_CHUNKS
    xspec = pl.BlockSpec((None, tm, d), lambda i, j: (i, j, 0))
    single = pl.Buffered(1)
    return pl.pallas_call(
        functools.partial(_ffn_kernel, tm=tm, final=final),
        grid=(b, t // tm),
        in_specs=[xspec,
                  pl.BlockSpec((None, 1, d), lambda i, j: (layer, 0, 0)),
                  pl.BlockSpec((None, nch, d, FF_CHUNK), lambda i, j: (layer, 0, 0, 0), pipeline_mode=single),
                  pl.BlockSpec((None, nch, d, FF_CHUNK), lambda i, j: (layer, 0, 0, 0), pipeline_mode=single),
                  pl.BlockSpec((None, nch, 3, FF_CHUNK), lambda i, j: (layer, 0, 0, 0)),
                  pl.BlockSpec((None, nch, 3, FF_CHUNK), lambda i, j: (layer, 0, 0, 0)),
                  pl.BlockSpec((None, nch, FF_CHUNK, d), lambda i, j: (layer, 0, 0, 0), pipeline_mode=single),
                  pl.BlockSpec((1, d), lambda i, j: (0, 0))],
        out_specs=[xspec,
                   pl.BlockSpec((None, nch, 2, SUBLANES, FF_CHUNK), lambda i, j: (i, 0, 0, 0, 0))],
        out_shape=[jax.ShapeDtypeStruct((b, t, d), F32),
                   jax.ShapeDtypeStruct((b, nch, 2, SUBLANES, FF_CHUNK), F32)],
        scratch_shapes=[pltpu.VMEM((tm, d), BF16),
                        pltpu.VMEM((tm, d), F32),
                        pltpu.VMEM((tm + SUBLANES, FF_CHUNK), F32),
                        pltpu.VMEM((tm + SUBLANES, FF_CHUNK), F32),
                        pltpu.VMEM((nch, SUBLANES, FF_CHUNK), F32),
                        pltpu.VMEM((nch, SUBLANES, FF_CHUNK), F32)],
        compiler_params=_params(("arbitrary", "arbitrary")),
        name="ffn",
    )(x, norm3, wa, wg, cwa, cwg, wd, gf2)


def _dec_in_kernel(x_ref, g_ref, w_ref, cw_ref, hist_ref, cos_ref, sa_ref, sb_ref,
                   q_ref, k_ref, v_ref, co_ref, nh_ref):
    h = _rms(x_ref[...], g_ref[...]).astype(BF16)
    cos, sa, sb = cos_ref[...], sa_ref[...], sb_ref[...]

    def proj(i):
        return _dot(h, w_ref[:, i * ATTN_WIDTH:(i + 1) * ATTN_WIDTH])

    q_ref[...] = _rope(proj(0), cos, sa, sb) * (DIFF_HEAD_DIM ** -0.5)
    k_ref[...] = _rope(proj(1), cos, sa, sb)
    v_ref[...] = proj(2)
    gate_b = proj(3)
    cu = proj(4) * proj(5)
    h0 = hist_ref[:, 0:CONV_WIDTH]
    h1 = hist_ref[:, CONV_WIDTH:2 * CONV_WIDTH]
    cw = cw_ref[...]
    y = cw[0:1] * h0 + cw[1:2] * h1 + cw[2:3] * cu
    co_ref[...] = (gate_b * y).astype(BF16)
    nh_ref[:, 0:CONV_WIDTH] = h1
    nh_ref[:, CONV_WIDTH:2 * CONV_WIDTH] = cu


def _dec_in(x, norm3, w_in, conv_w, hist, tables, layer):
    n, d = x.shape
    cos, sa, sb = tables
    full = lambda shape: pl.BlockSpec(shape, lambda i: (0,) * len(shape))
    ospec = full((n, ATTN_WIDTH))
    return pl.pallas_call(
        _dec_in_kernel,
        grid=(1,),
        in_specs=[full((n, d)),
                  pl.BlockSpec((None, 1, d), lambda i: (layer, 0, 0)),
                  pl.BlockSpec((None, d, w_in.shape[2]), lambda i: (layer, 0, 0)),
                  pl.BlockSpec((None, 3, CONV_WIDTH), lambda i: (layer, 0, 0)),
                  pl.BlockSpec((None, n, 2 * CONV_WIDTH), lambda i: (layer, 0, 0)),
                  full((1, LANES)), full((1, LANES)), full((1, LANES))],
        out_specs=[ospec, ospec, ospec, ospec, full((n, 2 * CONV_WIDTH))],
        out_shape=[jax.ShapeDtypeStruct((n, ATTN_WIDTH), F32)] * 3
        + [jax.ShapeDtypeStruct((n, CONV_WIDTH), BF16),
           jax.ShapeDtypeStruct((n, 2 * CONV_WIDTH), F32)],
        compiler_params=_params(("arbitrary",)),
        name="dec_in",
    )(x, norm3, w_in, conv_w, hist, cos, sa, sb)


def _group_select(n_rows, width, group):
    r = lax.broadcasted_iota(jnp.int32, (n_rows, width), 0)
    c = lax.broadcasted_iota(jnp.int32, (n_rows, width), 1)
    return r == c // group


def _dec_attn_kernel(pt_ref, q_ref, kn_ref, vn_ref, lq1_ref, lk1_ref, lq2_ref, lk2_ref, sg_ref,
                     ck_hbm, cv_hbm, o_ref, buf, sem, s_sc, p_sc, acc_sc,
                     *, layer, lam_init, n_req, n_pages, page):
    cp = PAGES_PER_CHUNK
    nch = n_pages // cp
    n_tok = cp * page
    width = ATTN_WIDTH
    jobs = 2 * nch

    def copies(b, job, slot):
        src = ck_hbm if job < nch else cv_hbm
        c = job % nch
        return [pltpu.make_async_copy(src.at[layer, pt_ref[b, c * cp + g]],
                                      buf.at[slot, pl.ds(g * page, page)],
                                      sem.at[slot]) for g in range(cp)]

    def start(b, job, slot):
        for cpy in copies(b, job, slot):
            cpy.start()

    def wait(b, job, slot):
        for cpy in copies(b, job, slot):
            cpy.wait()

    lam = _lam(lq1_ref[...], lk1_ref[...], lq2_ref[...], lk2_ref[...], lam_init)
    sg = sg_ref[...]
    lane = lax.broadcasted_iota(jnp.int32, (1, LANES), 1)
    q_sel = _group_select(LANES, width, DIFF_HEAD_DIM)
    r = lax.broadcasted_iota(jnp.int32, (LANES, width), 0)
    c = lax.broadcasted_iota(jnp.int32, (LANES, width), 1)
    expand = jnp.where(r // 2 == c // V_HEAD_DIM, 1.0, 0.0).astype(BF16)
    row0 = lax.broadcasted_iota(jnp.int32, (SUBLANES, LANES), 0) == 0

    start(0, 0, 0)

    def request(b, carry):
        qrow = q_ref[b]
        qbd = jnp.where(q_sel, jnp.broadcast_to(qrow, (LANES, width)), 0.0).astype(BF16)
        kn = jnp.broadcast_to(kn_ref[b], (SUBLANES, width)).astype(BF16)
        vn = vn_ref[b]
        s_self = _dot_nt(kn, qbd)[0:1, :]

        for job in range(jobs):
            slot = job % 2
            if job + 1 < jobs:
                start(b, job + 1, 1 - slot)
            else:
                @pl.when(b + 1 < n_req)
                def _():
                    start(b + 1, 0, 1 - slot)
            wait(b, job, slot)
            if job < nch:
                s_sc[job * n_tok:(job + 1) * n_tok, :] = _dot_nt(buf[slot].astype(BF16), qbd)
                if job == nch - 1:
                    s = s_sc[...]
                    m = jnp.maximum(jnp.max(s, axis=0, keepdims=True), s_self)
                    p = jnp.exp(s - m)
                    p_self = jnp.exp(s_self - m)
                    l = jnp.sum(p, axis=0, keepdims=True) + p_self
                    scale = jnp.where(lane % 2 == 0, 1.0, -lam) / l
                    p_sc[...] = (p * scale).astype(BF16)
                    pn_self = jnp.where(row0, jnp.broadcast_to(p_self * scale, (SUBLANES, LANES)), 0.0)
                    acc_sc[...] = _dot(pn_self.astype(BF16), expand) * vn
            else:
                base = (job - nch) * n_tok

                def page_body(g, acc):
                    off = pl.multiple_of(g * page, page)
                    rep = _dot(p_sc[pl.ds(base + off, page), :], expand)
                    pv = rep * buf[slot, pl.ds(off, page), :]
                    return acc + jnp.sum(pv.reshape(page // SUBLANES, SUBLANES, width), axis=0)

                acc_sc[...] = lax.fori_loop(0, cp, page_body, acc_sc[...])

        o = jnp.sum(acc_sc[...], axis=0, keepdims=True)
        o_ref[b] = jnp.concatenate(
            [_rms(o[:, h * V_HEAD_DIM:(h + 1) * V_HEAD_DIM], sg) * (1.0 - lam_init)
             for h in range(N_DIFF_HEADS)], axis=1)
        return carry

    lax.fori_loop(0, n_req, request, 0)


def _dec_attn(page_table, q, kn, vn, lams, sg3, cache_k, cache_v, layer, lam_init):
    n_req, n_pages = page_table.shape
    page = cache_k.shape[2]
    n_tok = PAGES_PER_CHUNK * page
    vmem = pl.BlockSpec(memory_space=pltpu.VMEM)
    smem = pl.BlockSpec(memory_space=pltpu.SMEM)
    hbm = pl.BlockSpec(memory_space=pl.ANY)
    return pl.pallas_call(
        functools.partial(_dec_attn_kernel, layer=layer, lam_init=lam_init, n_req=n_req,
                          n_pages=n_pages, page=page),
        in_specs=[smem, vmem, vmem, vmem, vmem, vmem, vmem, vmem, vmem, hbm, hbm],
        out_specs=vmem,
        out_shape=jax.ShapeDtypeStruct((n_req, 1, ATTN_WIDTH), F32),
        scratch_shapes=[pltpu.VMEM((2, n_tok, ATTN_WIDTH), F32),
                        pltpu.SemaphoreType.DMA((2,)),
                        pltpu.VMEM((n_pages * page, LANES), F32),
                        pltpu.VMEM((n_pages * page, LANES), BF16),
                        pltpu.VMEM((SUBLANES, ATTN_WIDTH), F32)],
        compiler_params=pltpu.CompilerParams(vmem_limit_bytes=VMEM_LIMIT_BYTES),
        name="dec_attn",
    )(page_table, *[a.reshape(n_req, 1, ATTN_WIDTH) for a in (q, kn, vn)],
      *[a[layer] for a in lams], sg3[layer], cache_k, cache_v).reshape(n_req, ATTN_WIDTH)


def _dec_mid_kernel(x_ref, at_ref, co_ref, wo_ref, g_ref, wq_ref, x1_ref, qc_ref):
    x1 = (x_ref[...] + _dot(at_ref[...].astype(BF16), wo_ref[0:ATTN_WIDTH, :])
          + _dot(co_ref[...], wo_ref[ATTN_WIDTH:ATTN_WIDTH + CONV_WIDTH, :]))
    x1_ref[...] = x1
    hq = _rms(x1, g_ref[...]).astype(BF16)
    qc_ref[...] = _dot(hq, wq_ref[...]) * (MEM_HEAD_DIM ** -0.5)


def _dec_mid(x, at, co, w_out, norm3, w_cq, layer):
    n, d = x.shape
    full = lambda shape: pl.BlockSpec(shape, lambda i: (0,) * len(shape))
    wspec = pl.BlockSpec((None, d, d), lambda i: (layer, 0, 0))
    return pl.pallas_call(
        _dec_mid_kernel,
        grid=(1,),
        in_specs=[full((n, d)), full((n, ATTN_WIDTH)), full((n, CONV_WIDTH)), wspec,
                  pl.BlockSpec((None, 1, d), lambda i: (layer, 0, 0)), wspec],
        out_specs=[full((n, d)), full((n, d))],
        out_shape=[jax.ShapeDtypeStruct((n, d), F32)] * 2,
        compiler_params=_params(("arbitrary",)),
        name="dec_mid",
    )(x, at, co, w_out, norm3, w_cq)


def _dec_cross_kernel(q_ref, mk_ref, mv_ref, o_ref):
    d = q_ref.shape[-1]
    q_sel = _group_select(LANES, d, MEM_HEAD_DIM)
    qbd = jnp.where(q_sel, jnp.broadcast_to(q_ref[...], (LANES, d)), 0.0).astype(BF16)
    s = _dot_nt(mk_ref[...].astype(BF16), qbd)
    m = jnp.max(s, axis=0, keepdims=True)
    p = jnp.exp(s - m)
    l = jnp.sum(p, axis=0, keepdims=True)
    expand = jnp.where(q_sel, 1.0, 0.0).astype(BF16)
    rep = _dot((p / l).astype(BF16), expand)
    o_ref[...] = jnp.sum(rep * mv_ref[...], axis=0, keepdims=True)


def _dec_cross(qc3, mem_k, mem_v, layer):
    n, _, d = qc3.shape
    n_mem = mem_k.shape[2]
    qspec = pl.BlockSpec((None, 1, d), lambda i: (i, 0, 0))
    mspec = pl.BlockSpec((None, None, n_mem, d), lambda i: (layer, i, 0, 0))
    return pl.pallas_call(
        _dec_cross_kernel,
        grid=(n,),
        in_specs=[qspec, mspec, mspec],
        out_specs=qspec,
        out_shape=jax.ShapeDtypeStruct((n, 1, d), F32),
        compiler_params=_params(("arbitrary",)),
        name="dec_cross",
    )(qc3, mem_k, mem_v)


def _dec_ffn_kernel(x1_ref, oc_ref, wco_ref, g_ref, wa_ref, wg_ref, cwa_ref, cwg_ref, wd_ref,
                    hist_ref, gf_ref, o_ref, nh_ref, *, final):
    x2 = x1_ref[...] + _dot(oc_ref[...].astype(BF16), wco_ref[...])
    hb = _rms(x2, g_ref[...]).astype(BF16)
    acc = x2
    for c in range(N_FF_CHUNKS):
        ys = []
        for half, (w_ref, cw_ref) in enumerate(((wa_ref, cwa_ref), (wg_ref, cwg_ref))):
            lo = half * D_FF + c * FF_CHUNK
            u = _dot(hb, w_ref[c])
            h0 = hist_ref[:, lo:lo + FF_CHUNK]
            h1 = hist_ref[:, 2 * D_FF + lo:2 * D_FF + lo + FF_CHUNK]
            cw = cw_ref[c]
            ys.append(cw[0:1] * h0 + cw[1:2] * h1 + cw[2:3] * u)
            nh_ref[:, lo:lo + FF_CHUNK] = h1
            nh_ref[:, 2 * D_FF + lo:2 * D_FF + lo + FF_CHUNK] = u
        act = (jax.nn.silu(ys[1]) * ys[0]).astype(BF16)
        acc = acc + _dot(act, wd_ref[c])
    if final:
        acc = _rms(acc, gf_ref[...])
    o_ref[...] = acc


def _dec_ffn(x1, oc, w_co, norm3, wa, wg, cwa, cwg, wd, hist, gf2, layer, final):
    n, d = x1.shape
    nch = N_FF_CHUNKS
    full = lambda shape: pl.BlockSpec(shape, lambda i: (0,) * len(shape))
    return pl.pallas_call(
        functools.partial(_dec_ffn_kernel, final=final),
        grid=(1,),
        in_specs=[full((n, d)), full((n, d)),
                  pl.BlockSpec((None, d, d), lambda i: (layer, 0, 0)),
                  pl.BlockSpec((None, 1, d), lambda i: (layer, 0, 0)),
                  pl.BlockSpec((None, nch, d, FF_CHUNK), lambda i: (layer, 0, 0, 0)),
                  pl.BlockSpec((None, nch, d, FF_CHUNK), lambda i: (layer, 0, 0, 0)),
                  pl.BlockSpec((None, nch, 3, FF_CHUNK), lambda i: (layer, 0, 0, 0)),
                  pl.BlockSpec((None, nch, 3, FF_CHUNK), lambda i: (layer, 0, 0, 0)),
                  pl.BlockSpec((None, nch, FF_CHUNK, d), lambda i: (layer, 0, 0, 0)),
                  pl.BlockSpec((None, n, 4 * D_FF), lambda i: (layer, 0, 0)),
                  full((1, d))],
        out_specs=[full((n, d)), full((n, 4 * D_FF))],
        out_shape=[jax.ShapeDtypeStruct((n, d), F32), jax.ShapeDtypeStruct((n, 4 * D_FF), F32)],
        compiler_params=_params(("arbitrary",)),
        name="dec_ffn",
    )(x1, oc, w_co, norm3, wa, wg, cwa, cwg, wd, hist, gf2)


def kernel(x_prompt, x_sample, mem_prompt, cache_k, cache_v, state_conv, state_ffn_conv,
           cache_mem_k, cache_mem_v, page_table, w_in, w_out, conv_w, lambda_q1, lambda_k1,
           lambda_q2, lambda_k2, subln_g, norm_mix, norm_cross, norm_mem, w_cq, w_ck, w_cv,
           w_co, norm_ffn, w_up, ffn_conv_w, w_down, norm_final):
    depth = w_in.shape[0]
    bp, tp, d = x_prompt.shape
    bs, ts, _ = x_sample.shape
    n_pages = page_table.shape[1]
    page = cache_k.shape[2]
    past_len = n_pages * page
    n_phys = cache_k.shape[1]
    nch = N_FF_CHUNKS

    bf = lambda w: w.astype(BF16)
    w_in_b, w_out_b, w_cq_b, w_ck_b, w_cv_b, w_co_b = map(bf, (w_in, w_out, w_cq, w_ck, w_cv, w_co))
    chunked = lambda w: bf(w).reshape(depth, d, nch, FF_CHUNK).transpose(0, 2, 1, 3)
    wa_b = chunked(w_up[:, :, :D_FF])
    wg_b = chunked(w_up[:, :, D_FF:])
    wd_b = bf(w_down).reshape(depth, nch, FF_CHUNK, d)
    cw_chunked = lambda w: w.reshape(depth, 3, nch, FF_CHUNK).transpose(0, 2, 1, 3)
    cwa = cw_chunked(ffn_conv_w[:, :, :D_FF])
    cwg = cw_chunked(ffn_conv_w[:, :, D_FF:])
    row3 = lambda g: g.reshape(depth, 1, g.shape[-1])
    norm_mix3, norm_cross3, norm_mem3, norm_ffn3, subln3 = map(
        row3, (norm_mix, norm_cross, norm_mem, norm_ffn, subln_g))
    lams = tuple(map(row3, (lambda_q1, lambda_k1, lambda_q2, lambda_k2)))
    gf2 = norm_final.reshape(1, d)

    tables_p = _rope_tables(jnp.arange(tp))
    tables_s = _rope_tables(past_len + jnp.arange(ts))

    cache_k4 = cache_k.reshape(depth, n_phys, page, ATTN_WIDTH)
    cache_v4 = cache_v.reshape(depth, n_phys, page, ATTN_WIDTH)
    mem_k4 = cache_mem_k.reshape(depth, bs, cache_mem_k.shape[2], d)
    mem_v4 = cache_mem_v.reshape(depth, bs, cache_mem_v.shape[2], d)
    conv_hist = state_conv.reshape(depth, bs, 2 * CONV_WIDTH)
    ffn_hist = state_ffn_conv.reshape(depth, bs, 4 * D_FF)

    mk, mv, mkb, mvb = _memkv(mem_prompt, norm_mem3, w_ck_b, w_cv_b)

    xp = x_prompt
    xs = x_sample.reshape(bs * ts, d)
    kp_l, vp_l, cp_l, fp_l, ks_l, vs_l, cs_l, fs_l = ([] for _ in range(8))
    for l in range(depth):
        lam_init = 0.8 - 0.6 * math.exp(-0.3 * l)
        final = l == depth - 1
        q1, q2, k, v, kb, vb, co, chist = _mix_in(xp, norm_mix3, w_in_b, conv_w, tables_p, l)
        at = _attn(q1, q2, kb, vb, lams, subln3, l, lam_init)
        x2 = _mix_out(xp, at, co, w_out_b, norm_cross3, w_cq_b, mkb, mvb, w_co_b, l)
        xp, fhist = _ffn(x2, norm_ffn3, wa_b, wg_b, cwa, cwg, wd_b, gf2, l, final)
        kp_l.append(k)
        vp_l.append(v)
        cp_l.append(chist[:, SUBLANES - 2:, :])
        fp_l.append(fhist[:, :, :, SUBLANES - 2:, :].transpose(0, 3, 2, 1, 4).reshape(bp, 2, 2 * D_FF))
        qd, kd, vd, cod, nconv = _dec_in(xs, norm_mix3, w_in_b, conv_w, conv_hist, tables_s, l)
        atd = _dec_attn(page_table, qd, kd, vd, lams, subln3, cache_k4, cache_v4, l, lam_init)
        x1d, qc = _dec_mid(xs, atd, cod, w_out_b, norm_cross3, w_cq_b, l)
        oc = _dec_cross(qc.reshape(bs, 1, d), mem_k4, mem_v4, l).reshape(bs, d)
        xs, nffn = _dec_ffn(x1d, oc, w_co_b, norm_ffn3, wa_b, wg_b, cwa, cwg, wd_b, ffn_hist, gf2, l, final)
        ks_l.append(kd)
        vs_l.append(vd)
        cs_l.append(nconv)
        fs_l.append(nffn)

    st = jnp.stack
    return (xp,
            xs.reshape(bs, ts, d),
            st(kp_l).reshape(depth, bp, tp, N_DIFF_HEADS, 2, DIFF_HEAD_DIM),
            st(vp_l).reshape(depth, bp, tp, N_DIFF_HEADS, V_HEAD_DIM),
            st(cp_l),
            st(fp_l),
            mk.reshape(depth, bp, -1, N_MEM_HEADS, MEM_HEAD_DIM),
            mv.reshape(depth, bp, -1, N_MEM_HEADS, MEM_HEAD_DIM),
            st(ks_l).reshape(depth, bs, ts, N_DIFF_HEADS, 2, DIFF_HEAD_DIM),
            st(vs_l).reshape(depth, bs, ts, N_DIFF_HEADS, V_HEAD_DIM),
            st(cs_l).reshape(depth, bs, 2, CONV_WIDTH),
            st(fs_l).reshape(depth, bs, 2, 2 * D_FF))
```

```python
import functools
import math

import jax
import jax.numpy as jnp
from jax import lax
from jax.experimental import pallas as pl
from jax.experimental.pallas import tpu as pltpu

F32 = jnp.float32
BF16 = jnp.bfloat16

D_MODEL = 1024
ATTN_WIDTH = 512
CONV_WIDTH = 512
N_DIFF_HEADS = 4
DIFF_HEAD_DIM = 64
V_HEAD_DIM = 2 * DIFF_HEAD_DIM
ROT_DIM = DIFF_HEAD_DIM // 4
ROPE_THETA = 500000.0
N_MEM_HEADS = 4
MEM_HEAD_DIM = D_MODEL // N_MEM_HEADS
D_FF = 11 * D_MODEL // 4
EPS = 1e-6
NEG_INF = -1e30
LOG2E = math.log2(math.e)

LANES = 128
SUBLANES = 8
VMEM_LIMIT_BYTES = 56 * 1024 * 1024

TM_MIX = 512
TQ = 512
TM_FFN = 512
FF_CHUNK = 256
N_FF_CHUNKS = D_FF // FF_CHUNK
PAGES_PER_CHUNK = 16


def _dot(a, b):
    return jnp.dot(a, b, preferred_element_type=F32)


def _dot_nt(a, b):
    return lax.dot_general(a, b, (((1,), (1,)), ((), ())), preferred_element_type=F32)


def _rms(x, g):
    ms = jnp.mean(x * x, axis=-1, keepdims=True)
    return x * lax.rsqrt(ms + EPS) * g


def _rope(y, cos, sa, sb):
    outs = []
    for c in range(y.shape[1] // LANES):
        yc = y[:, c * LANES:(c + 1) * LANES]
        outs.append(yc * cos + pltpu.roll(yc, ROT_DIM // 2, 1) * sa
                    + pltpu.roll(yc, LANES - ROT_DIM // 2, 1) * sb)
    return jnp.concatenate(outs, axis=1)


def _rope_tables(pos):
    half = ROT_DIM // 2
    inv = jnp.power(ROPE_THETA, -jnp.arange(half, dtype=F32) * 2.0 / ROT_DIM)
    ang = pos.astype(F32)[:, None] * inv[None, :]
    cos, sin = jnp.cos(ang), jnp.sin(ang)
    n = pos.shape[0]
    ones = jnp.ones((n, DIFF_HEAD_DIM - ROT_DIM), F32)
    zeros = jnp.zeros_like(ones)
    z8 = jnp.zeros((n, half), F32)
    c64 = jnp.concatenate([cos, cos, ones], axis=1)
    sa64 = jnp.concatenate([z8, sin, zeros], axis=1)
    sb64 = jnp.concatenate([-sin, z8, zeros], axis=1)
    tile2 = lambda a: jnp.concatenate([a, a], axis=1)
    return tile2(c64), tile2(sa64), tile2(sb64)


def _lam(lq1, lk1, lq2, lk2, lam_init):
    return (jnp.exp(jnp.sum(lq1 * lk1, axis=-1, keepdims=True))
            - jnp.exp(jnp.sum(lq2 * lk2, axis=-1, keepdims=True)) + lam_init)


def _params(sem=None):
    return pltpu.CompilerParams(dimension_semantics=sem, vmem_limit_bytes=VMEM_LIMIT_BYTES)


def _memkv_kernel(mem_ref, g_ref, wk_ref, wv_ref, mk_ref, mv_ref, mkb_ref, mvb_ref):
    hm = _rms(mem_ref[...], g_ref[...]).astype(BF16)
    mk = _dot(hm, wk_ref[...])
    mv = _dot(hm, wv_ref[...])
    mk_ref[...] = mk
    mv_ref[...] = mv
    mkb_ref[...] = mk.astype(BF16)
    mvb_ref[...] = mv.astype(BF16)


def _memkv(mem, g3, wk, wv):
    depth = wk.shape[0]
    b, n, d = mem.shape
    wspec = pl.BlockSpec((None, d, d), lambda l, i: (l, 0, 0))
    ospec = pl.BlockSpec((None, None, n, d), lambda l, i: (l, i, 0, 0))
    return pl.pallas_call(
        _memkv_kernel,
        grid=(depth, b),
        in_specs=[pl.BlockSpec((None, n, d), lambda l, i: (i, 0, 0)),
                  pl.BlockSpec((None, 1, d), lambda l, i: (l, 0, 0)),
                  wspec, wspec],
        out_specs=[ospec, ospec, ospec, ospec],
        out_shape=[jax.ShapeDtypeStruct((depth, b, n, d), F32)] * 2
        + [jax.ShapeDtypeStruct((depth, b, n, d), BF16)] * 2,
        compiler_params=_params(("arbitrary", "arbitrary")),
        name="memkv",
    )(mem, g3, wk, wv)


def _mix_in_kernel(x_ref, g_ref, w_ref, cw_ref, cos_ref, sa_ref, sb_ref,
                   q1_ref, q2_ref, k_ref, v_ref, kb_ref, vt_ref, co_ref, hist_ref,
                   cbuf, *, tm):
    t = pl.program_id(1)
    h = _rms(x_ref[...], g_ref[...]).astype(BF16)
    cos, sa, sb = cos_ref[...], sa_ref[...], sb_ref[...]

    def proj(i):
        return _dot(h, w_ref[:, i * ATTN_WIDTH:(i + 1) * ATTN_WIDTH])

    lane = lax.broadcasted_iota(jnp.int32, (tm, ATTN_WIDTH), 1)
    first_map = (lane % V_HEAD_DIM) < DIFF_HEAD_DIM
    q = _rope(proj(0), cos, sa, sb) * (DIFF_HEAD_DIM ** -0.5 * LOG2E)
    q1_ref[...] = jnp.where(first_map, q, 0.0).astype(BF16)
    q2_ref[...] = jnp.where(first_map, 0.0, q).astype(BF16)
    k = _rope(proj(1), cos, sa, sb)
    k_ref[...] = k
    kb_ref[...] = k.astype(BF16)
    v = proj(2)
    v_ref[...] = v
    vt_ref[...] = v.T.astype(BF16)

    gate_b = proj(3)
    cu = proj(4) * proj(5)

    @pl.when(t == 0)
    def _():
        cbuf[0:SUBLANES, :] = jnp.zeros((SUBLANES, CONV_WIDTH), F32)

    cbuf[SUBLANES:SUBLANES + tm, :] = cu
    cw = cw_ref[...]
    y = (cw[0:1] * cbuf[SUBLANES - 2:SUBLANES - 2 + tm, :]
         + cw[1:2] * cbuf[SUBLANES - 1:SUBLANES - 1 + tm, :]
         + cw[2:3] * cu)
    co_ref[...] = (gate_b * y).astype(BF16)
    tail = cu[tm - SUBLANES:, :]
    hist_ref[...] = tail
    cbuf[0:SUBLANES, :] = tail


def _mix_in(x, norm3, w_in, conv_w, tables, layer):
    b, t, d = x.shape
    tm = TM_MIX
    cos, sa, sb = tables
    tok = lambda w, dt: jax.ShapeDtypeStruct((b, t, w), dt)
    tspec = pl.BlockSpec((tm, LANES), lambda i, j: (j, 0))
    ospec = pl.BlockSpec((None, tm, ATTN_WIDTH), lambda i, j: (i, j, 0))
    return pl.pallas_call(
        functools.partial(_mix_in_kernel, tm=tm),
        grid=(b, t // tm),
        in_specs=[pl.BlockSpec((None, tm, d), lambda i, j: (i, j, 0)),
                  pl.BlockSpec((None, 1, d), lambda i, j: (layer, 0, 0)),
                  pl.BlockSpec((None, d, w_in.shape[2]), lambda i, j: (layer, 0, 0)),
                  pl.BlockSpec((None, 3, CONV_WIDTH), lambda i, j: (layer, 0, 0)),
                  tspec, tspec, tspec],
        out_specs=[ospec] * 5 + [pl.BlockSpec((None, ATTN_WIDTH, tm), lambda i, j: (i, 0, j)), ospec,
                                 pl.BlockSpec((None, SUBLANES, CONV_WIDTH), lambda i, j: (i, 0, 0))],
        out_shape=[tok(ATTN_WIDTH, BF16), tok(ATTN_WIDTH, BF16), tok(ATTN_WIDTH, F32), tok(ATTN_WIDTH, F32),
                   tok(ATTN_WIDTH, BF16), jax.ShapeDtypeStruct((b, ATTN_WIDTH, t), BF16), tok(CONV_WIDTH, BF16),
                   jax.ShapeDtypeStruct((b, SUBLANES, CONV_WIDTH), F32)],
        scratch_shapes=[pltpu.VMEM((tm + SUBLANES, CONV_WIDTH), F32)],
        compiler_params=_params(("arbitrary", "arbitrary")),
        name="mix_in",
    )(x, norm3, w_in, conv_w, cos, sa, sb)


def _attn_kernel(q1_ref, q2_ref, k_ref, vt_ref, lq1_ref, lk1_ref, lq2_ref, lk2_ref, sg_ref,
                 o_ref, s_sc, m_sc, l_sc, acc_sc, *, tq, lam_init):
    i = pl.program_id(1)
    j = pl.program_id(2)
    ones_rows = 16

    @pl.when(j == 0)
    def _():
        m_sc[...] = jnp.full(m_sc.shape, NEG_INF, F32)
        l_sc[...] = jnp.zeros(l_sc.shape, F32)
        acc_sc[...] = jnp.zeros(acc_sc.shape, F32)

    def step(masked):
        for h in range(N_DIFF_HEADS):
            hs = slice(h * V_HEAD_DIM, (h + 1) * V_HEAD_DIM)
            kh = k_ref[:, hs]
            for c, q_ref in enumerate((q1_ref, q2_ref)):
                s_sc[2 * h + c] = _dot_nt(kh, q_ref[:, hs])
        s = s_sc[...]
        if masked:
            key = lax.broadcasted_iota(jnp.int32, (tq, tq), 0)
            qry = lax.broadcasted_iota(jnp.int32, (tq, tq), 1)
            s = jnp.where((key <= qry)[None], s, NEG_INF)
        m_old = m_sc[...]
        m_new = jnp.maximum(m_old, jnp.max(s, axis=1, keepdims=True))
        alpha = jnp.exp2(m_old - m_new)
        p = jnp.exp2(s - m_new).astype(BF16)
        m_sc[...] = m_new
        ones = jnp.ones((ones_rows, tq), BF16)
        for h in range(N_DIFF_HEADS):
            lhs = jnp.concatenate([vt_ref[h * V_HEAD_DIM:(h + 1) * V_HEAD_DIM, :], ones], axis=0)
            for c in range(2):
                idx = 2 * h + c
                r = _dot(lhs, p[idx])
                acc_sc[idx] = alpha[idx] * acc_sc[idx] + r[0:V_HEAD_DIM]
                l_sc[idx] = alpha[idx] * l_sc[idx] + r[V_HEAD_DIM:V_HEAD_DIM + 1]

    @pl.when(j < i)
    def _():
        step(False)

    @pl.when(j == i)
    def _():
        step(True)
        lam = _lam(lq1_ref[...], lk1_ref[...], lq2_ref[...], lk2_ref[...], lam_init)
        sg = sg_ref[...]
        for h in range(N_DIFF_HEADS):
            o = acc_sc[2 * h] / l_sc[2 * h] - lam * (acc_sc[2 * h + 1] / l_sc[2 * h + 1])
            ms = jnp.mean(o * o, axis=0, keepdims=True)
            o = o * lax.rsqrt(ms + EPS) * sg * (1.0 - lam_init)
            o_ref[:, h * V_HEAD_DIM:(h + 1) * V_HEAD_DIM] = o.T.astype(BF16)


def _attn(q1, q2, kb, vt, lams, sg_col, layer, lam_init):
    b, t, w = q1.shape
    tq = TQ
    nq = t // tq
    n_combo = 2 * N_DIFF_HEADS
    qspec = pl.BlockSpec((None, tq, w), lambda bi, i, j: (bi, i, 0))
    kspec = pl.BlockSpec((None, tq, w), lambda bi, i, j: (bi, jnp.minimum(i, j), 0))
    vspec = pl.BlockSpec((None, w, tq), lambda bi, i, j: (bi, 0, jnp.minimum(i, j)))
    lspec = pl.BlockSpec((None, 1, DIFF_HEAD_DIM), lambda bi, i, j: (layer, 0, 0))
    return pl.pallas_call(
        functools.partial(_attn_kernel, tq=tq, lam_init=lam_init),
        grid=(b, nq, nq),
        in_specs=[qspec, qspec, kspec, vspec, lspec, lspec, lspec, lspec,
                  pl.BlockSpec((None, V_HEAD_DIM, 1), lambda bi, i, j: (layer, 0, 0))],
        out_specs=pl.BlockSpec((None, tq, w), lambda bi, i, j: (bi, i, 0)),
        out_shape=jax.ShapeDtypeStruct((b, t, w), BF16),
        scratch_shapes=[pltpu.VMEM((n_combo, tq, tq), F32),
                        pltpu.VMEM((n_combo, 1, tq), F32),
                        pltpu.VMEM((n_combo, 1, tq), F32),
                        pltpu.VMEM((n_combo, V_HEAD_DIM, tq), F32)],
        compiler_params=_params(("arbitrary", "arbitrary", "arbitrary")),
        name="diff_attn",
    )(q1, q2, kb, vt, *lams, sg_col)


def _cross_attend(qc, mk_ref, mv_ref):
    outs = []
    for h in range(N_MEM_HEADS):
        hs = slice(h * MEM_HEAD_DIM, (h + 1) * MEM_HEAD_DIM)
        s = _dot_nt(qc[:, hs], mk_ref[:, hs])
        m = jnp.max(s, axis=1, keepdims=True)
        p = jnp.exp(s - m)
        l = jnp.sum(p, axis=1, keepdims=True)
        outs.append((_dot(p.astype(BF16), mv_ref[:, hs]) / l).astype(BF16))
    return jnp.concatenate(outs, axis=1)


def _mix_out_kernel(x_ref, at_ref, co_ref, wo_ref, g_ref, wq_ref, mk_ref, mv_ref, wco_ref, o_ref):
    x1 = (x_ref[...] + _dot(at_ref[...], wo_ref[0:ATTN_WIDTH, :])
          + _dot(co_ref[...], wo_ref[ATTN_WIDTH:ATTN_WIDTH + CONV_WIDTH, :]))
    hq = _rms(x1, g_ref[...]).astype(BF16)
    qc = (_dot(hq, wq_ref[...]) * (MEM_HEAD_DIM ** -0.5)).astype(BF16)
    o = _cross_attend(qc, mk_ref, mv_ref)
    o_ref[...] = x1 + _dot(o, wco_ref[...])


def _mix_out(x, at, co, w_out, norm3, w_cq, mkb, mvb, w_co, layer):
    b, t, d = x.shape
    tm = TM_MIX
    n_mem = mkb.shape[2]
    wspec = pl.BlockSpec((None, d, d), lambda i, j: (layer, 0, 0))
    hspec = pl.BlockSpec((None, tm, ATTN_WIDTH), lambda i, j: (i, j, 0))
    mspec = pl.BlockSpec((None, None, n_mem, d), lambda i, j: (layer, i, 0, 0))
    xspec = pl.BlockSpec((None, tm, d), lambda i, j: (i, j, 0))
    return pl.pallas_call(
        _mix_out_kernel,
        grid=(b, t // tm),
        in_specs=[xspec, hspec, hspec, wspec,
                  pl.BlockSpec((None, 1, d), lambda i, j: (layer, 0, 0)),
                  wspec, mspec, mspec, wspec],
        out_specs=xspec,
        out_shape=jax.ShapeDtypeStruct((b, t, d), F32),
        compiler_params=_params(("arbitrary", "arbitrary")),
        name="mix_out",
    )(x, at, co, w_out, norm3, w_cq, mkb, mvb, w_co)


def _ffn_kernel(x_ref, g_ref, wa_ref, wg_ref, cwa_ref, cwg_ref, wd_ref, gf_ref,
                o_ref, hist_ref, hb_sc, acc_sc, buf_a, buf_g, car_a, car_g, *, tm, final):
    t = pl.program_id(1)
    x = x_ref[...]
    hb_sc[...] = _rms(x, g_ref[...]).astype(BF16)
    acc_sc[...] = x

    @pl.when(t == 0)
    def _():
        car_a[...] = jnp.zeros(car_a.shape, F32)
        car_g[...] = jnp.zeros(car_g.shape, F32)

    def conv(u, buf, car, cw, c):
        buf[0:SUBLANES, :] = car[c]
        buf[SUBLANES:SUBLANES + tm, :] = u
        y = (cw[0:1] * buf[SUBLANES - 2:SUBLANES - 2 + tm, :]
             + cw[1:2] * buf[SUBLANES - 1:SUBLANES - 1 + tm, :]
             + cw[2:3] * u)
        car[c] = u[tm - SUBLANES:, :]
        return y

    def body(c, carry):
        hb = hb_sc[...]
        a = conv(_dot(hb, wa_ref[c]), buf_a, car_a, cwa_ref[c], c)
        g = conv(_dot(hb, wg_ref[c]), buf_g, car_g, cwg_ref[c], c)
        act = (jax.nn.silu(g) * a).astype(BF16)
        acc_sc[...] += _dot(act, wd_ref[c])
        return carry

    lax.fori_loop(0, N_FF_CHUNKS, body, 0)
    hist_ref[:, 0] = car_a[...]
    hist_ref[:, 1] = car_g[...]
    out = acc_sc[...]
    if final:
        out = _rms(out, gf_ref[...])
    o_ref[...] = out


def _ffn(x, norm3, wa, wg, cwa, cwg, wd, gf2, layer, final):
    b, t, d = x.shape
    tm = TM_FFN
    nch = N_FF_CHUNKS
    xspec = pl.BlockSpec((None, tm, d), lambda i, j: (i, j, 0))
    single = pl.Buffered(1)
    return pl.pallas_call(
        functools.partial(_ffn_kernel, tm=tm, final=final),
        grid=(b, t // tm),
        in_specs=[xspec,
                  pl.BlockSpec((None, 1, d), lambda i, j: (layer, 0, 0)),
                  pl.BlockSpec((None, nch, d, FF_CHUNK), lambda i, j: (layer, 0, 0, 0), pipeline_mode=single),
                  pl.BlockSpec((None, nch, d, FF_CHUNK), lambda i, j: (layer, 0, 0, 0), pipeline_mode=single),
                  pl.BlockSpec((None, nch, 3, FF_CHUNK), lambda i, j: (layer, 0, 0, 0)),
                  pl.BlockSpec((None, nch, 3, FF_CHUNK), lambda i, j: (layer, 0, 0, 0)),
                  pl.BlockSpec((None, nch, FF_CHUNK, d), lambda i, j: (layer, 0, 0, 0), pipeline_mode=single),
                  pl.BlockSpec((1, d), lambda i, j: (0, 0))],
        out_specs=[xspec,
                   pl.BlockSpec((None, nch, 2, SUBLANES, FF_CHUNK), lambda i, j: (i, 0, 0, 0, 0))],
        out_shape=[jax.ShapeDtypeStruct((b, t, d), F32),
                   jax.ShapeDtypeStruct((b, nch, 2, SUBLANES, FF_CHUNK), F32)],
        scratch_shapes=[pltpu.VMEM((tm, d), BF16),
                        pltpu.VMEM((tm, d), F32),
                        pltpu.VMEM((tm + SUBLANES, FF_CHUNK), F32),
                        pltpu.VMEM((tm + SUBLANES, FF_CHUNK), F32),
                        pltpu.VMEM((nch, SUBLANES, FF_CHUNK), F32),
                        pltpu.VMEM((nch, SUBLANES, FF_CHUNK), F32)],
        compiler_params=_params(("arbitrary", "arbitrary")),
        name="ffn",
    )(x, norm3, wa, wg, cwa, cwg, wd, gf2)


def _dec_in_kernel(x_ref, g_ref, w_ref, cw_ref, hist_ref, cos_ref, sa_ref, sb_ref,
                   q_ref, k_ref, v_ref, co_ref, nh_ref):
    h = _rms(x_ref[...], g_ref[...]).astype(BF16)
    cos, sa, sb = cos_ref[...], sa_ref[...], sb_ref[...]

    def proj(i):
        return _dot(h, w_ref[:, i * ATTN_WIDTH:(i + 1) * ATTN_WIDTH])

    q_ref[...] = _rope(proj(0), cos, sa, sb) * (DIFF_HEAD_DIM ** -0.5)
    k_ref[...] = _rope(proj(1), cos, sa, sb)
    v_ref[...] = proj(2)
    gate_b = proj(3)
    cu = proj(4) * proj(5)
    h0 = hist_ref[:, 0:CONV_WIDTH]
    h1 = hist_ref[:, CONV_WIDTH:2 * CONV_WIDTH]
    cw = cw_ref[...]
    y = cw[0:1] * h0 + cw[1:2] * h1 + cw[2:3] * cu
    co_ref[...] = (gate_b * y).astype(BF16)
    nh_ref[:, 0:CONV_WIDTH] = h1
    nh_ref[:, CONV_WIDTH:2 * CONV_WIDTH] = cu


def _dec_in(x, norm3, w_in, conv_w, hist, tables, layer):
    n, d = x.shape
    cos, sa, sb = tables
    full = lambda shape: pl.BlockSpec(shape, lambda i: (0,) * len(shape))
    ospec = full((n, ATTN_WIDTH))
    return pl.pallas_call(
        _dec_in_kernel,
        grid=(1,),
        in_specs=[full((n, d)),
                  pl.BlockSpec((None, 1, d), lambda i: (layer, 0, 0)),
                  pl.BlockSpec((None, d, w_in.shape[2]), lambda i: (layer, 0, 0)),
                  pl.BlockSpec((None, 3, CONV_WIDTH), lambda i: (layer, 0, 0)),
                  pl.BlockSpec((None, n, 2 * CONV_WIDTH), lambda i: (layer, 0, 0)),
                  full((1, LANES)), full((1, LANES)), full((1, LANES))],
        out_specs=[ospec, ospec, ospec, ospec, full((n, 2 * CONV_WIDTH))],
        out_shape=[jax.ShapeDtypeStruct((n, ATTN_WIDTH), F32)] * 3
        + [jax.ShapeDtypeStruct((n, CONV_WIDTH), BF16),
           jax.ShapeDtypeStruct((n, 2 * CONV_WIDTH), F32)],
        compiler_params=_params(("arbitrary",)),
        name="dec_in",
    )(x, norm3, w_in, conv_w, hist, cos, sa, sb)


def _group_select(n_rows, width, group):
    r = lax.broadcasted_iota(jnp.int32, (n_rows, width), 0)
    c = lax.broadcasted_iota(jnp.int32, (n_rows, width), 1)
    return r == c // group


def _dec_attn_kernel(pt_ref, q_ref, kn_ref, vn_ref, lq1_ref, lk1_ref, lq2_ref, lk2_ref, sg_ref,
                     kt_hbm, v_hbm, o_ref, buf, sem, s_sc, p_sc, x_sc,
                     *, layer, lam_init, n_req, n_pages, page):
    cp = PAGES_PER_CHUNK
    nch = n_pages // cp
    width = ATTN_WIDTH
    n_combo = 2 * N_DIFF_HEADS
    jobs = 2 * nch

    def copies(b, job, slot):
        src = kt_hbm if job < nch else v_hbm
        c = job % nch
        return [pltpu.make_async_copy(src.at[layer, pt_ref[b, c * cp + g]], buf.at[slot, g], sem.at[slot])
                for g in range(cp)]

    def start(b, job, slot):
        for cpy in copies(b, job, slot):
            cpy.start()

    def wait(b, job, slot):
        for cpy in copies(b, job, slot):
            cpy.wait()

    lam = _lam(lq1_ref[...], lk1_ref[...], lq2_ref[...], lk2_ref[...], lam_init)
    sg = sg_ref[...]
    combo = lax.broadcasted_iota(jnp.int32, (n_combo, 1), 0)
    feat_sel = _group_select(n_combo, width, DIFF_HEAD_DIM)
    r = lax.broadcasted_iota(jnp.int32, (page, N_DIFF_HEADS * page), 0)
    c = lax.broadcasted_iota(jnp.int32, (page, N_DIFF_HEADS * page), 1)
    dilate = jnp.where(c // N_DIFF_HEADS == r, 1.0, 0.0).astype(BF16)
    rk = lax.broadcasted_iota(jnp.int32, (cp * n_combo, N_DIFF_HEADS * page), 0) % n_combo
    ch = lax.broadcasted_iota(jnp.int32, (cp * n_combo, N_DIFF_HEADS * page), 1) % N_DIFF_HEADS
    head_sel = rk // 2 == ch

    start(0, 0, 0)

    def request(b, carry):
        qrow = q_ref[b]
        qcol = jnp.broadcast_to(qrow, (page, width)).T
        vn = vn_ref[b]
        s_self = jnp.sum(jnp.where(feat_sel, jnp.broadcast_to(qrow * kn_ref[b], (n_combo, width)), 0.0),
                         axis=1, keepdims=True)
        o8 = None
        for job in range(jobs):
            slot = job % 2
            if job + 1 < jobs:
                start(b, job + 1, 1 - slot)
            else:
                @pl.when(b + 1 < n_req)
                def _():
                    start(b + 1, 0, 1 - slot)
            wait(b, job, slot)
            if job < nch:
                def k_page(g, carry2, job=job, slot=slot):
                    prod = buf[slot, g] * qcol
                    s_sc[job * cp + g] = jnp.sum(prod.reshape(n_combo, DIFF_HEAD_DIM, page), axis=1)
                    return carry2

                lax.fori_loop(0, cp, k_page, 0)
                if job == nch - 1:
                    s = s_sc[...]
                    m = jnp.maximum(jnp.max(jnp.max(s, axis=0), axis=1, keepdims=True), s_self)
                    p = jnp.exp(s - m)
                    p_self = jnp.exp(s_self - m)
                    l = jnp.sum(jnp.sum(p, axis=0), axis=1, keepdims=True) + p_self
                    scale = jnp.where(combo % 2 == 0, 1.0, -lam) / l
                    p_sc[...] = p * scale
                    vn8 = jnp.concatenate(
                        [vn[:, (k // 2) * V_HEAD_DIM:(k // 2 + 1) * V_HEAD_DIM] for k in range(n_combo)], axis=0)
                    o8 = (p_self * scale) * vn8
            else:
                jv = job - nch
                pch = p_sc[jv * cp:(jv + 1) * cp].reshape(cp * n_combo, page).astype(BF16)
                x_sc[...] = jnp.where(head_sel, _dot(pch, dilate), 0.0)

                def v_page(g, acc, slot=slot):
                    xg = x_sc[pl.ds(pl.multiple_of(g * n_combo, n_combo), n_combo), :].astype(BF16)
                    return acc + _dot(xg, buf[slot, g].astype(BF16))

                o8 = lax.fori_loop(0, cp, v_page, o8, unroll=4)

        o_ref[b] = jnp.concatenate(
            [_rms(o8[2 * h:2 * h + 1] + o8[2 * h + 1:2 * h + 2], sg) * (1.0 - lam_init)
             for h in range(N_DIFF_HEADS)], axis=1)
        return carry

    lax.fori_loop(0, n_req, request, 0)


def _dec_attn(page_table, q, kn, vn, lams, sg3, cache_kt, cache_v2, layer, lam_init):
    n_req, n_pages = page_table.shape
    page = cache_kt.shape[3]
    cp = PAGES_PER_CHUNK
    n_combo = 2 * N_DIFF_HEADS
    vmem = pl.BlockSpec(memory_space=pltpu.VMEM)
    smem = pl.BlockSpec(memory_space=pltpu.SMEM)
    hbm = pl.BlockSpec(memory_space=pl.ANY)
    return pl.pallas_call(
        functools.partial(_dec_attn_kernel, layer=layer, lam_init=lam_init, n_req=n_req,
                          n_pages=n_pages, page=page),
        in_specs=[smem, vmem, vmem, vmem, vmem, vmem, vmem, vmem, vmem, hbm, hbm],
        out_specs=vmem,
        out_shape=jax.ShapeDtypeStruct((n_req, 1, ATTN_WIDTH), F32),
        scratch_shapes=[pltpu.VMEM((2, cp, ATTN_WIDTH, page), F32),
                        pltpu.SemaphoreType.DMA((2,)),
                        pltpu.VMEM((n_pages, n_combo, page), F32),
                        pltpu.VMEM((n_pages, n_combo, page), F32),
                        pltpu.VMEM((cp * n_combo, N_DIFF_HEADS * page), F32)],
        compiler_params=pltpu.CompilerParams(vmem_limit_bytes=VMEM_LIMIT_BYTES),
        name="dec_attn",
    )(page_table, *[a.reshape(n_req, 1, ATTN_WIDTH) for a in (q, kn, vn)],
      *[a[layer] for a in lams], sg3[layer], cache_kt, cache_v2).reshape(n_req, ATTN_WIDTH)


def _dec_mid_kernel(x_ref, at_ref, co_ref, wo_ref, g_ref, wq_ref, x1_ref, qc_ref):
    x1 = (x_ref[...] + _dot(at_ref[...].astype(BF16), wo_ref[0:ATTN_WIDTH, :])
          + _dot(co_ref[...], wo_ref[ATTN_WIDTH:ATTN_WIDTH + CONV_WIDTH, :]))
    x1_ref[...] = x1
    hq = _rms(x1, g_ref[...]).astype(BF16)
    qc_ref[...] = _dot(hq, wq_ref[...]) * (MEM_HEAD_DIM ** -0.5)


def _dec_mid(x, at, co, w_out, norm3, w_cq, layer):
    n, d = x.shape
    full = lambda shape: pl.BlockSpec(shape, lambda i: (0,) * len(shape))
    wspec = pl.BlockSpec((None, d, d), lambda i: (layer, 0, 0))
    return pl.pallas_call(
        _dec_mid_kernel,
        grid=(1,),
        in_specs=[full((n, d)), full((n, ATTN_WIDTH)), full((n, CONV_WIDTH)), wspec,
                  pl.BlockSpec((None, 1, d), lambda i: (layer, 0, 0)), wspec],
        out_specs=[full((n, d)), full((n, d))],
        out_shape=[jax.ShapeDtypeStruct((n, d), F32)] * 2,
        compiler_params=_params(("arbitrary",)),
        name="dec_mid",
    )(x, at, co, w_out, norm3, w_cq)


def _dec_cross_kernel(q_ref, mk_ref, mv_ref, o_ref):
    d = q_ref.shape[-1]
    q_sel = _group_select(LANES, d, MEM_HEAD_DIM)
    qbd = jnp.where(q_sel, jnp.broadcast_to(q_ref[...], (LANES, d)), 0.0).astype(BF16)
    s = _dot_nt(mk_ref[...].astype(BF16), qbd)
    m = jnp.max(s, axis=0, keepdims=True)
    p = jnp.exp(s - m)
    l = jnp.sum(p, axis=0, keepdims=True)
    expand = jnp.where(q_sel, 1.0, 0.0).astype(BF16)
    rep = _dot((p / l).astype(BF16), expand)
    o_ref[...] = jnp.sum(rep * mv_ref[...], axis=0, keepdims=True)


def _dec_cross(qc3, mem_k, mem_v, layer):
    n, _, d = qc3.shape
    n_mem = mem_k.shape[2]
    qspec = pl.BlockSpec((None, 1, d), lambda i: (i, 0, 0))
    mspec = pl.BlockSpec((None, None, n_mem, d), lambda i: (layer, i, 0, 0))
    return pl.pallas_call(
        _dec_cross_kernel,
        grid=(n,),
        in_specs=[qspec, mspec, mspec],
        out_specs=qspec,
        out_shape=jax.ShapeDtypeStruct((n, 1, d), F32),
        compiler_params=_params(("arbitrary",)),
        name="dec_cross",
    )(qc3, mem_k, mem_v)


def _dec_ffn_kernel(x1_ref, oc_ref, wco_ref, g_ref, wa_ref, wg_ref, cwa_ref, cwg_ref, wd_ref,
                    hist_ref, gf_ref, o_ref, nh_ref, *, final):
    x2 = x1_ref[...] + _dot(oc_ref[...].astype(BF16), wco_ref[...])
    hb = _rms(x2, g_ref[...]).astype(BF16)
    acc = x2
    for c in range(N_FF_CHUNKS):
        ys = []
        for half, (w_ref, cw_ref) in enumerate(((wa_ref, cwa_ref), (wg_ref, cwg_ref))):
            lo = half * D_FF + c * FF_CHUNK
            u = _dot(hb, w_ref[c])
            h0 = hist_ref[:, lo:lo + FF_CHUNK]
            h1 = hist_ref[:, 2 * D_FF + lo:2 * D_FF + lo + FF_CHUNK]
            cw = cw_ref[c]
            ys.append(cw[0:1] * h0 + cw[1:2] * h1 + cw[2:3] * u)
            nh_ref[:, lo:lo + FF_CHUNK] = h1
            nh_ref[:, 2 * D_FF + lo:2 * D_FF + lo + FF_CHUNK] = u
        act = (jax.nn.silu(ys[1]) * ys[0]).astype(BF16)
        acc = acc + _dot(act, wd_ref[c])
    if final:
        acc = _rms(acc, gf_ref[...])
    o_ref[...] = acc


def _dec_ffn(x1, oc, w_co, norm3, wa, wg, cwa, cwg, wd, hist, gf2, layer, final):
    n, d = x1.shape
    nch = N_FF_CHUNKS
    full = lambda shape: pl.BlockSpec(shape, lambda i: (0,) * len(shape))
    return pl.pallas_call(
        functools.partial(_dec_ffn_kernel, final=final),
        grid=(1,),
        in_specs=[full((n, d)), full((n, d)),
                  pl.BlockSpec((None, d, d), lambda i: (layer, 0, 0)),
                  pl.BlockSpec((None, 1, d), lambda i: (layer, 0, 0)),
                  pl.BlockSpec((None, nch, d, FF_CHUNK), lambda i: (layer, 0, 0, 0)),
                  pl.BlockSpec((None, nch, d, FF_CHUNK), lambda i: (layer, 0, 0, 0)),
                  pl.BlockSpec((None, nch, 3, FF_CHUNK), lambda i: (layer, 0, 0, 0)),
                  pl.BlockSpec((None, nch, 3, FF_CHUNK), lambda i: (layer, 0, 0, 0)),
                  pl.BlockSpec((None, nch, FF_CHUNK, d), lambda i: (layer, 0, 0, 0)),
                  pl.BlockSpec((None, n, 4 * D_FF), lambda i: (layer, 0, 0)),
                  full((1, d))],
        out_specs=[full((n, d)), full((n, 4 * D_FF))],
        out_shape=[jax.ShapeDtypeStruct((n, d), F32), jax.ShapeDtypeStruct((n, 4 * D_FF), F32)],
        compiler_params=_params(("arbitrary",)),
        name="dec_ffn",
    )(x1, oc, w_co, norm3, wa, wg, cwa, cwg, wd, hist, gf2)


def kernel(x_prompt, x_sample, mem_prompt, cache_k, cache_v, state_conv, state_ffn_conv,
           cache_mem_k, cache_mem_v, page_table, w_in, w_out, conv_w, lambda_q1, lambda_k1,
           lambda_q2, lambda_k2, subln_g, norm_mix, norm_cross, norm_mem, w_cq, w_ck, w_cv,
           w_co, norm_ffn, w_up, ffn_conv_w, w_down, norm_final):
    depth = w_in.shape[0]
    bp, tp, d = x_prompt.shape
    bs, ts, _ = x_sample.shape
    n_pages = page_table.shape[1]
    page = cache_k.shape[2]
    past_len = n_pages * page
    n_phys = cache_k.shape[1]
    nch = N_FF_CHUNKS

    bf = lambda w: w.astype(BF16)
    w_in_b, w_out_b, w_cq_b, w_ck_b, w_cv_b, w_co_b = map(bf, (w_in, w_out, w_cq, w_ck, w_cv, w_co))
    chunked = lambda w: bf(w).reshape(depth, d, nch, FF_CHUNK).transpose(0, 2, 1, 3)
    wa_b = chunked(w_up[:, :, :D_FF])
    wg_b = chunked(w_up[:, :, D_FF:])
    wd_b = bf(w_down).reshape(depth, nch, FF_CHUNK, d)
    cw_chunked = lambda w: w.reshape(depth, 3, nch, FF_CHUNK).transpose(0, 2, 1, 3)
    cwa = cw_chunked(ffn_conv_w[:, :, :D_FF])
    cwg = cw_chunked(ffn_conv_w[:, :, D_FF:])
    row3 = lambda g: g.reshape(depth, 1, g.shape[-1])
    norm_mix3, norm_cross3, norm_mem3, norm_ffn3, subln3 = map(
        row3, (norm_mix, norm_cross, norm_mem, norm_ffn, subln_g))
    lams = tuple(map(row3, (lambda_q1, lambda_k1, lambda_q2, lambda_k2)))
    gf2 = norm_final.reshape(1, d)
    subln_col = subln_g.reshape(depth, V_HEAD_DIM, 1)

    tables_p = _rope_tables(jnp.arange(tp))
    tables_s = _rope_tables(past_len + jnp.arange(ts))

    cache_k4 = cache_k.transpose(0, 1, 3, 4, 5, 2).reshape(depth, n_phys, ATTN_WIDTH, page)
    cache_v4 = cache_v.reshape(depth, n_phys, N_DIFF_HEADS * page, V_HEAD_DIM)
    mem_k4 = cache_mem_k.reshape(depth, bs, cache_mem_k.shape[2], d)
    mem_v4 = cache_mem_v.reshape(depth, bs, cache_mem_v.shape[2], d)
    conv_hist = state_conv.reshape(depth, bs, 2 * CONV_WIDTH)
    ffn_hist = state_ffn_conv.reshape(depth, bs, 4 * D_FF)

    mk, mv, mkb, mvb = _memkv(mem_prompt, norm_mem3, w_ck_b, w_cv_b)

    xp = x_prompt
    xs = x_sample.reshape(bs * ts, d)
    kp_l, vp_l, cp_l, fp_l, ks_l, vs_l, cs_l, fs_l = ([] for _ in range(8))
    for l in range(depth):
        lam_init = 0.8 - 0.6 * math.exp(-0.3 * l)
        final = l == depth - 1
        q1, q2, k, v, kb, vt, co, chist = _mix_in(xp, norm_mix3, w_in_b, conv_w, tables_p, l)
        at = _attn(q1, q2, kb, vt, lams, subln_col, l, lam_init)
        x2 = _mix_out(xp, at, co, w_out_b, norm_cross3, w_cq_b, mkb, mvb, w_co_b, l)
        xp, fhist = _ffn(x2, norm_ffn3, wa_b, wg_b, cwa, cwg, wd_b, gf2, l, final)
        kp_l.append(k)
        vp_l.append(v)
        cp_l.append(chist[:, SUBLANES - 2:, :])
        fp_l.append(fhist[:, :, :, SUBLANES - 2:, :].transpose(0, 3, 2, 1, 4).reshape(bp, 2, 2 * D_FF))
        qd, kd, vd, cod, nconv = _dec_in(xs, norm_mix3, w_in_b, conv_w, conv_hist, tables_s, l)
        atd = _dec_attn(page_table, qd, kd, vd, lams, subln3, cache_k4, cache_v4, l, lam_init)
        x1d, qc = _dec_mid(xs, atd, cod, w_out_b, norm_cross3, w_cq_b, l)
        oc = _dec_cross(qc.reshape(bs, 1, d), mem_k4, mem_v4, l).reshape(bs, d)
        xs, nffn = _dec_ffn(x1d, oc, w_co_b, norm_ffn3, wa_b, wg_b, cwa, cwg, wd_b, ffn_hist, gf2, l, final)
        ks_l.append(kd)
        vs_l.append(vd)
        cs_l.append(nconv)
        fs_l.append(nffn)

    st = jnp.stack
    return (xp,
            xs.reshape(bs, ts, d),
            st(kp_l).reshape(depth, bp, tp, N_DIFF_HEADS, 2, DIFF_HEAD_DIM),
            st(vp_l).reshape(depth, bp, tp, N_DIFF_HEADS, V_HEAD_DIM),
            st(cp_l),
            st(fp_l),
            mk.reshape(depth, bp, -1, N_MEM_HEADS, MEM_HEAD_DIM),
            mv.reshape(depth, bp, -1, N_MEM_HEADS, MEM_HEAD_DIM),
            st(ks_l).reshape(depth, bs, ts, N_DIFF_HEADS, 2, DIFF_HEAD_DIM),
            st(vs_l).reshape(depth, bs, ts, N_DIFF_HEADS, V_HEAD_DIM),
            st(cs_l).reshape(depth, bs, 2, CONV_WIDTH),
            st(fs_l).reshape(depth, bs, 2, 2 * D_FF))
```

```python
import functools
import math

import jax
import jax.numpy as jnp
from jax import lax
from jax.experimental import pallas as pl
from jax.experimental.pallas import tpu as pltpu

F32 = jnp.float32
BF16 = jnp.bfloat16

D_MODEL = 1024
ATTN_WIDTH = 512
CONV_WIDTH = 512
N_DIFF_HEADS = 4
DIFF_HEAD_DIM = 64
V_HEAD_DIM = 2 * DIFF_HEAD_DIM
ROT_DIM = DIFF_HEAD_DIM // 4
ROPE_THETA = 500000.0
N_MEM_HEADS = 4
MEM_HEAD_DIM = D_MODEL // N_MEM_HEADS
D_FF = 11 * D_MODEL // 4
EPS = 1e-6
NEG_INF = -1e30
LOG2E = math.log2(math.e)

LANES = 128
SUBLANES = 8
VMEM_LIMIT_BYTES = 56 * 1024 * 1024

TM_MIX = 512
TQ = 512
TM_FFN = 1024
FF_CHUNK = 256
N_FF_CHUNKS = D_FF // FF_CHUNK
PAGES_PER_CHUNK = 16
DEC_ATTN_SLOTS = 3


def _dot(a, b):
    return jnp.dot(a, b, preferred_element_type=F32)


def _dot_nt(a, b):
    return lax.dot_general(a, b, (((1,), (1,)), ((), ())), preferred_element_type=F32)


def _rms(x, g):
    ms = jnp.mean(x * x, axis=-1, keepdims=True)
    return x * lax.rsqrt(ms + EPS) * g


def _rope(y, cos, sa, sb):
    outs = []
    for c in range(y.shape[1] // LANES):
        yc = y[:, c * LANES:(c + 1) * LANES]
        outs.append(yc * cos + pltpu.roll(yc, ROT_DIM // 2, 1) * sa
                    + pltpu.roll(yc, LANES - ROT_DIM // 2, 1) * sb)
    return jnp.concatenate(outs, axis=1)


def _rope_tables(pos):
    half = ROT_DIM // 2
    inv = jnp.power(ROPE_THETA, -jnp.arange(half, dtype=F32) * 2.0 / ROT_DIM)
    ang = pos.astype(F32)[:, None] * inv[None, :]
    cos, sin = jnp.cos(ang), jnp.sin(ang)
    n = pos.shape[0]
    ones = jnp.ones((n, DIFF_HEAD_DIM - ROT_DIM), F32)
    zeros = jnp.zeros_like(ones)
    z8 = jnp.zeros((n, half), F32)
    c64 = jnp.concatenate([cos, cos, ones], axis=1)
    sa64 = jnp.concatenate([z8, sin, zeros], axis=1)
    sb64 = jnp.concatenate([-sin, z8, zeros], axis=1)
    tile2 = lambda a: jnp.concatenate([a, a], axis=1)
    return tile2(c64), tile2(sa64), tile2(sb64)


def _lam(lq1, lk1, lq2, lk2, lam_init):
    return (jnp.exp(jnp.sum(lq1 * lk1, axis=-1, keepdims=True))
            - jnp.exp(jnp.sum(lq2 * lk2, axis=-1, keepdims=True)) + lam_init)


def _params(sem=None):
    return pltpu.CompilerParams(dimension_semantics=sem, vmem_limit_bytes=VMEM_LIMIT_BYTES)


def _memkv_kernel(mem_ref, g_ref, wk_ref, wv_ref, mk_ref, mv_ref, mkb_ref, mvb_ref):
    hm = _rms(mem_ref[...], g_ref[...]).astype(BF16)
    mk = _dot(hm, wk_ref[...])
    mv = _dot(hm, wv_ref[...])
    mk_ref[...] = mk
    mv_ref[...] = mv
    mkb_ref[...] = mk.astype(BF16)
    mvb_ref[...] = mv.astype(BF16)


def _memkv(mem, g3, wk, wv):
    depth = wk.shape[0]
    b, n, d = mem.shape
    wspec = pl.BlockSpec((None, d, d), lambda l, i: (l, 0, 0))
    ospec = pl.BlockSpec((None, None, n, d), lambda l, i: (l, i, 0, 0))
    return pl.pallas_call(
        _memkv_kernel,
        grid=(depth, b),
        in_specs=[pl.BlockSpec((None, n, d), lambda l, i: (i, 0, 0)),
                  pl.BlockSpec((None, 1, d), lambda l, i: (l, 0, 0)),
                  wspec, wspec],
        out_specs=[ospec, ospec, ospec, ospec],
        out_shape=[jax.ShapeDtypeStruct((depth, b, n, d), F32)] * 2
        + [jax.ShapeDtypeStruct((depth, b, n, d), BF16)] * 2,
        compiler_params=_params(("arbitrary", "arbitrary")),
        name="memkv",
    )(mem, g3, wk, wv)


def _mix_in_kernel(x_ref, g_ref, w_ref, cw_ref, cos_ref, sa_ref, sb_ref,
                   q1_ref, q2_ref, k_ref, v_ref, kb_ref, vt_ref, co_ref, hist_ref,
                   cbuf, *, tm):
    t = pl.program_id(1)
    h = _rms(x_ref[...], g_ref[...]).astype(BF16)
    cos, sa, sb = cos_ref[...], sa_ref[...], sb_ref[...]

    def proj(i):
        return _dot(h, w_ref[:, i * ATTN_WIDTH:(i + 1) * ATTN_WIDTH])

    lane = lax.broadcasted_iota(jnp.int32, (tm, ATTN_WIDTH), 1)
    first_map = (lane % V_HEAD_DIM) < DIFF_HEAD_DIM
    q = _rope(proj(0), cos, sa, sb) * (DIFF_HEAD_DIM ** -0.5 * LOG2E)
    q1_ref[...] = jnp.where(first_map, q, 0.0).astype(BF16)
    q2_ref[...] = jnp.where(first_map, 0.0, q).astype(BF16)
    k = _rope(proj(1), cos, sa, sb)
    k_ref[...] = k
    kb_ref[...] = k.astype(BF16)
    v = proj(2)
    v_ref[...] = v
    vt_ref[...] = v.T.astype(BF16)

    gate_b = proj(3)
    cu = proj(4) * proj(5)

    @pl.when(t == 0)
    def _():
        cbuf[0:SUBLANES, :] = jnp.zeros((SUBLANES, CONV_WIDTH), F32)

    cbuf[SUBLANES:SUBLANES + tm, :] = cu
    cw = cw_ref[...]
    y = (cw[0:1] * cbuf[SUBLANES - 2:SUBLANES - 2 + tm, :]
         + cw[1:2] * cbuf[SUBLANES - 1:SUBLANES - 1 + tm, :]
         + cw[2:3] * cu)
    co_ref[...] = (gate_b * y).astype(BF16)
    tail = cu[tm - SUBLANES:, :]
    hist_ref[...] = tail
    cbuf[0:SUBLANES, :] = tail


def _mix_in(x, norm3, w_in, conv_w, tables, layer):
    b, t, d = x.shape
    tm = TM_MIX
    cos, sa, sb = tables
    tok = lambda w, dt: jax.ShapeDtypeStruct((b, t, w), dt)
    tspec = pl.BlockSpec((tm, LANES), lambda i, j: (j, 0))
    ospec = pl.BlockSpec((None, tm, ATTN_WIDTH), lambda i, j: (i, j, 0))
    return pl.pallas_call(
        functools.partial(_mix_in_kernel, tm=tm),
        grid=(b, t // tm),
        in_specs=[pl.BlockSpec((None, tm, d), lambda i, j: (i, j, 0)),
                  pl.BlockSpec((None, 1, d), lambda i, j: (layer, 0, 0)),
                  pl.BlockSpec((None, d, w_in.shape[2]), lambda i, j: (layer, 0, 0)),
                  pl.BlockSpec((None, 3, CONV_WIDTH), lambda i, j: (layer, 0, 0)),
                  tspec, tspec, tspec],
        out_specs=[ospec] * 5 + [pl.BlockSpec((None, ATTN_WIDTH, tm), lambda i, j: (i, 0, j)), ospec,
                                 pl.BlockSpec((None, SUBLANES, CONV_WIDTH), lambda i, j: (i, 0, 0))],
        out_shape=[tok(ATTN_WIDTH, BF16), tok(ATTN_WIDTH, BF16), tok(ATTN_WIDTH, F32), tok(ATTN_WIDTH, F32),
                   tok(ATTN_WIDTH, BF16), jax.ShapeDtypeStruct((b, ATTN_WIDTH, t), BF16), tok(CONV_WIDTH, BF16),
                   jax.ShapeDtypeStruct((b, SUBLANES, CONV_WIDTH), F32)],
        scratch_shapes=[pltpu.VMEM((tm + SUBLANES, CONV_WIDTH), F32)],
        compiler_params=_params(("arbitrary", "arbitrary")),
        name="mix_in",
    )(x, norm3, w_in, conv_w, cos, sa, sb)


def _attn_kernel(q1_ref, q2_ref, k_ref, vt_ref, lq1_ref, lk1_ref, lq2_ref, lk2_ref, sg_ref,
                 o_ref, s_sc, m_sc, l_sc, acc_sc, *, tq, lam_init):
    i = pl.program_id(1)
    j = pl.program_id(2)
    ones_rows = 16

    @pl.when(j == 0)
    def _():
        m_sc[...] = jnp.full(m_sc.shape, NEG_INF, F32)
        l_sc[...] = jnp.zeros(l_sc.shape, F32)
        acc_sc[...] = jnp.zeros(acc_sc.shape, F32)

    def step(masked):
        for h in range(N_DIFF_HEADS):
            hs = slice(h * V_HEAD_DIM, (h + 1) * V_HEAD_DIM)
            kh = k_ref[:, hs]
            for c, q_ref in enumerate((q1_ref, q2_ref)):
                s_sc[2 * h + c] = _dot_nt(kh, q_ref[:, hs])
        s = s_sc[...]
        if masked:
            key = lax.broadcasted_iota(jnp.int32, (tq, tq), 0)
            qry = lax.broadcasted_iota(jnp.int32, (tq, tq), 1)
            s = jnp.where((key <= qry)[None], s, NEG_INF)
        m_old = m_sc[...]
        m_new = jnp.maximum(m_old, jnp.max(s, axis=1, keepdims=True))
        alpha = jnp.exp2(m_old - m_new)
        p = jnp.exp2(s - m_new).astype(BF16)
        m_sc[...] = m_new
        ones = jnp.ones((ones_rows, tq), BF16)
        for h in range(N_DIFF_HEADS):
            lhs = jnp.concatenate([vt_ref[h * V_HEAD_DIM:(h + 1) * V_HEAD_DIM, :], ones], axis=0)
            for c in range(2):
                idx = 2 * h + c
                r = _dot(lhs, p[idx])
                acc_sc[idx] = alpha[idx] * acc_sc[idx] + r[0:V_HEAD_DIM]
                l_sc[idx] = alpha[idx] * l_sc[idx] + r[V_HEAD_DIM:V_HEAD_DIM + 1]

    @pl.when(j < i)
    def _():
        step(False)

    @pl.when(j == i)
    def _():
        step(True)
        lam = _lam(lq1_ref[...], lk1_ref[...], lq2_ref[...], lk2_ref[...], lam_init)
        sg = sg_ref[...]
        for h in range(N_DIFF_HEADS):
            o = acc_sc[2 * h] / l_sc[2 * h] - lam * (acc_sc[2 * h + 1] / l_sc[2 * h + 1])
            ms = jnp.mean(o * o, axis=0, keepdims=True)
            o = o * lax.rsqrt(ms + EPS) * sg * (1.0 - lam_init)
            o_ref[:, h * V_HEAD_DIM:(h + 1) * V_HEAD_DIM] = o.T.astype(BF16)


def _attn(q1, q2, kb, vt, lams, sg_col, layer, lam_init):
    b, t, w = q1.shape
    tq = TQ
    nq = t // tq
    n_combo = 2 * N_DIFF_HEADS
    qspec = pl.BlockSpec((None, tq, w), lambda bi, i, j: (bi, i, 0))
    kspec = pl.BlockSpec((None, tq, w), lambda bi, i, j: (bi, jnp.minimum(i, j), 0))
    vspec = pl.BlockSpec((None, w, tq), lambda bi, i, j: (bi, 0, jnp.minimum(i, j)))
    lspec = pl.BlockSpec((None, 1, DIFF_HEAD_DIM), lambda bi, i, j: (layer, 0, 0))
    return pl.pallas_call(
        functools.partial(_attn_kernel, tq=tq, lam_init=lam_init),
        grid=(b, nq, nq),
        in_specs=[qspec, qspec, kspec, vspec, lspec, lspec, lspec, lspec,
                  pl.BlockSpec((None, V_HEAD_DIM, 1), lambda bi, i, j: (layer, 0, 0))],
        out_specs=pl.BlockSpec((None, tq, w), lambda bi, i, j: (bi, i, 0)),
        out_shape=jax.ShapeDtypeStruct((b, t, w), BF16),
        scratch_shapes=[pltpu.VMEM((n_combo, tq, tq), F32),
                        pltpu.VMEM((n_combo, 1, tq), F32),
                        pltpu.VMEM((n_combo, 1, tq), F32),
                        pltpu.VMEM((n_combo, V_HEAD_DIM, tq), F32)],
        compiler_params=_params(("arbitrary", "arbitrary", "arbitrary")),
        name="diff_attn",
    )(q1, q2, kb, vt, *lams, sg_col)


def _cross_attend(qc, mk_ref, mv_ref):
    outs = []
    for h in range(N_MEM_HEADS):
        hs = slice(h * MEM_HEAD_DIM, (h + 1) * MEM_HEAD_DIM)
        s = _dot_nt(qc[:, hs], mk_ref[:, hs])
        m = jnp.max(s, axis=1, keepdims=True)
        p = jnp.exp(s - m)
        l = jnp.sum(p, axis=1, keepdims=True)
        outs.append((_dot(p.astype(BF16), mv_ref[:, hs]) / l).astype(BF16))
    return jnp.concatenate(outs, axis=1)


def _mix_out_kernel(x_ref, at_ref, co_ref, wo_ref, g_ref, wq_ref, mk_ref, mv_ref, wco_ref, o_ref):
    x1 = (x_ref[...] + _dot(at_ref[...], wo_ref[0:ATTN_WIDTH, :])
          + _dot(co_ref[...], wo_ref[ATTN_WIDTH:ATTN_WIDTH + CONV_WIDTH, :]))
    hq = _rms(x1, g_ref[...]).astype(BF16)
    qc = (_dot(hq, wq_ref[...]) * (MEM_HEAD_DIM ** -0.5)).astype(BF16)
    o = _cross_attend(qc, mk_ref, mv_ref)
    o_ref[...] = x1 + _dot(o, wco_ref[...])


def _mix_out(x, at, co, w_out, norm3, w_cq, mkb, mvb, w_co, layer):
    b, t, d = x.shape
    tm = TM_MIX
    n_mem = mkb.shape[2]
    wspec = pl.BlockSpec((None, d, d), lambda i, j: (layer, 0, 0))
    hspec = pl.BlockSpec((None, tm, ATTN_WIDTH), lambda i, j: (i, j, 0))
    mspec = pl.BlockSpec((None, None, n_mem, d), lambda i, j: (layer, i, 0, 0))
    xspec = pl.BlockSpec((None, tm, d), lambda i, j: (i, j, 0))
    return pl.pallas_call(
        _mix_out_kernel,
        grid=(b, t // tm),
        in_specs=[xspec, hspec, hspec, wspec,
                  pl.BlockSpec((None, 1, d), lambda i, j: (layer, 0, 0)),
                  wspec, mspec, mspec, wspec],
        out_specs=xspec,
        out_shape=jax.ShapeDtypeStruct((b, t, d), F32),
        compiler_params=_params(("arbitrary", "arbitrary")),
        name="mix_out",
    )(x, at, co, w_out, norm3, w_cq, mkb, mvb, w_co)


def _ffn_kernel(x_ref, g_ref, wa_ref, wg_ref, cwa_ref, cwg_ref, wd_ref, gf_ref,
                o_ref, hist_ref, hb_sc, acc_sc, buf, car_a, car_g, *, tm, final):
    t = pl.program_id(1)
    x = x_ref[...]
    hb_sc[...] = _rms(x, g_ref[...]).astype(BF16)
    acc_sc[...] = x

    @pl.when(t == 0)
    def _():
        car_a[...] = jnp.zeros(car_a.shape, F32)
        car_g[...] = jnp.zeros(car_g.shape, F32)

    def conv(u, stage, car, cw, c):
        stage[0:SUBLANES, :] = car[c]
        stage[SUBLANES:SUBLANES + tm, :] = u
        y = (cw[0:1] * stage[SUBLANES - 2:SUBLANES - 2 + tm, :]
             + cw[1:2] * stage[SUBLANES - 1:SUBLANES - 1 + tm, :]
             + cw[2:3] * u)
        car[c] = u[tm - SUBLANES:, :]
        return y

    def chunk(c, pair_slot):
        hb = hb_sc[...]
        a = conv(_dot(hb, wa_ref[c]), buf.at[2 * pair_slot], car_a, cwa_ref[c], c)
        g = conv(_dot(hb, wg_ref[c]), buf.at[2 * pair_slot + 1], car_g, cwg_ref[c], c)
        act = (jax.nn.silu(g) * a).astype(BF16)
        return _dot(act, wd_ref[c])

    def pair(k, carry):
        acc_sc[...] += chunk(2 * k, 0) + chunk(2 * k + 1, 1)
        return carry

    lax.fori_loop(0, N_FF_CHUNKS // 2, pair, 0)
    for c in range(N_FF_CHUNKS - N_FF_CHUNKS % 2, N_FF_CHUNKS):
        acc_sc[...] += chunk(c, 0)
    hist_ref[:, 0] = car_a[...]
    hist_ref[:, 1] = car_g[...]
    out = acc_sc[...]
    if final:
        out = _rms(out, gf_ref[...])
    o_ref[...] = out


def _ffn(x, norm3, wa, wg, cwa, cwg, wd, gf2, layer, final):
    b, t, d = x.shape
    tm = TM_FFN
    nch = N_FF_CHUNKS
    xspec = pl.BlockSpec((None, tm, d), lambda i, j: (i, j, 0))
    single = pl.Buffered(1)
    return pl.pallas_call(
        functools.partial(_ffn_kernel, tm=tm, final=final),
        grid=(b, t // tm),
        in_specs=[xspec,
                  pl.BlockSpec((None, 1, d), lambda i, j: (layer, 0, 0)),
                  pl.BlockSpec((None, nch, d, FF_CHUNK), lambda i, j: (layer, 0, 0, 0), pipeline_mode=single),
                  pl.BlockSpec((None, nch, d, FF_CHUNK), lambda i, j: (layer, 0, 0, 0), pipeline_mode=single),
                  pl.BlockSpec((None, nch, 3, FF_CHUNK), lambda i, j: (layer, 0, 0, 0)),
                  pl.BlockSpec((None, nch, 3, FF_CHUNK), lambda i, j: (layer, 0, 0, 0)),
                  pl.BlockSpec((None, nch, FF_CHUNK, d), lambda i, j: (layer, 0, 0, 0), pipeline_mode=single),
                  pl.BlockSpec((1, d), lambda i, j: (0, 0))],
        out_specs=[xspec,
                   pl.BlockSpec((None, nch, 2, SUBLANES, FF_CHUNK), lambda i, j: (i, 0, 0, 0, 0))],
        out_shape=[jax.ShapeDtypeStruct((b, t, d), F32),
                   jax.ShapeDtypeStruct((b, nch, 2, SUBLANES, FF_CHUNK), F32)],
        scratch_shapes=[pltpu.VMEM((tm, d), BF16),
                        pltpu.VMEM((tm, d), F32),
                        pltpu.VMEM((4, tm + SUBLANES, FF_CHUNK), F32),
                        pltpu.VMEM((nch, SUBLANES, FF_CHUNK), F32),
                        pltpu.VMEM((nch, SUBLANES, FF_CHUNK), F32)],
        compiler_params=_params(("arbitrary", "arbitrary")),
        name="ffn",
    )(x, norm3, wa, wg, cwa, cwg, wd, gf2)


def _dec_in_kernel(x_ref, g_ref, w_ref, cw_ref, hist_ref, cos_ref, sa_ref, sb_ref,
                   q_ref, k_ref, v_ref, co_ref, nh_ref):
    h = _rms(x_ref[...], g_ref[...]).astype(BF16)
    cos, sa, sb = cos_ref[...], sa_ref[...], sb_ref[...]

    def proj(i):
        return _dot(h, w_ref[:, i * ATTN_WIDTH:(i + 1) * ATTN_WIDTH])

    q_ref[...] = _rope(proj(0), cos, sa, sb) * (DIFF_HEAD_DIM ** -0.5)
    k_ref[...] = _rope(proj(1), cos, sa, sb)
    v_ref[...] = proj(2)
    gate_b = proj(3)
    cu = proj(4) * proj(5)
    h0 = hist_ref[:, 0:CONV_WIDTH]
    h1 = hist_ref[:, CONV_WIDTH:2 * CONV_WIDTH]
    cw = cw_ref[...]
    y = cw[0:1] * h0 + cw[1:2] * h1 + cw[2:3] * cu
    co_ref[...] = (gate_b * y).astype(BF16)
    nh_ref[:, 0:CONV_WIDTH] = h1
    nh_ref[:, CONV_WIDTH:2 * CONV_WIDTH] = cu


def _dec_in(x, norm3, w_in, conv_w, hist, tables, layer):
    n, d = x.shape
    cos, sa, sb = tables
    full = lambda shape: pl.BlockSpec(shape, lambda i: (0,) * len(shape))
    ospec = full((n, ATTN_WIDTH))
    return pl.pallas_call(
        _dec_in_kernel,
        grid=(1,),
        in_specs=[full((n, d)),
                  pl.BlockSpec((None, 1, d), lambda i: (layer, 0, 0)),
                  pl.BlockSpec((None, d, w_in.shape[2]), lambda i: (layer, 0, 0)),
                  pl.BlockSpec((None, 3, CONV_WIDTH), lambda i: (layer, 0, 0)),
                  pl.BlockSpec((None, n, 2 * CONV_WIDTH), lambda i: (layer, 0, 0)),
                  full((1, LANES)), full((1, LANES)), full((1, LANES))],
        out_specs=[ospec, ospec, ospec, ospec, full((n, 2 * CONV_WIDTH))],
        out_shape=[jax.ShapeDtypeStruct((n, ATTN_WIDTH), F32)] * 3
        + [jax.ShapeDtypeStruct((n, CONV_WIDTH), BF16),
           jax.ShapeDtypeStruct((n, 2 * CONV_WIDTH), F32)],
        compiler_params=_params(("arbitrary",)),
        name="dec_in",
    )(x, norm3, w_in, conv_w, hist, cos, sa, sb)


def _group_select(n_rows, width, group):
    r = lax.broadcasted_iota(jnp.int32, (n_rows, width), 0)
    c = lax.broadcasted_iota(jnp.int32, (n_rows, width), 1)
    return r == c // group


def _dec_attn_kernel(pt_ref, q_ref, kn_ref, vn_ref, lq1_ref, lk1_ref, lq2_ref, lk2_ref, sg_ref,
                     kt_hbm, v_hbm, o_ref, buf, sem, s_sc, p_sc, x_sc,
                     *, layer, lam_init, n_req, n_pages, page):
    cp = PAGES_PER_CHUNK
    nch = n_pages // cp
    width = ATTN_WIDTH
    n_combo = 2 * N_DIFF_HEADS
    jobs = 2 * nch

    def copies(b, job, slot):
        src = kt_hbm if job < nch else v_hbm
        c = job % nch
        return [pltpu.make_async_copy(src.at[layer, pt_ref[b, c * cp + g]], buf.at[slot, g], sem.at[slot])
                for g in range(cp)]

    def start(b, job, slot):
        for cpy in copies(b, job, slot):
            cpy.start()

    def wait(b, job, slot):
        for cpy in copies(b, job, slot):
            cpy.wait()

    lam = _lam(lq1_ref[...], lk1_ref[...], lq2_ref[...], lk2_ref[...], lam_init)
    sg = sg_ref[...]
    combo = lax.broadcasted_iota(jnp.int32, (n_combo, 1), 0)
    feat_sel = _group_select(n_combo, width, DIFF_HEAD_DIM)
    r = lax.broadcasted_iota(jnp.int32, (page, N_DIFF_HEADS * page), 0)
    c = lax.broadcasted_iota(jnp.int32, (page, N_DIFF_HEADS * page), 1)
    dilate = jnp.where(c // N_DIFF_HEADS == r, 1.0, 0.0).astype(BF16)
    rk = lax.broadcasted_iota(jnp.int32, (cp * n_combo, N_DIFF_HEADS * page), 0) % n_combo
    ch = lax.broadcasted_iota(jnp.int32, (cp * n_combo, N_DIFF_HEADS * page), 1) % N_DIFF_HEADS
    head_sel = rk // 2 == ch

    ahead = DEC_ATTN_SLOTS - 1

    def slot_of(b, job):
        return lax.rem(b * jobs + job, DEC_ATTN_SLOTS)

    for first in range(ahead):
        start(0, first, first)

    def request(b, carry):
        qrow = q_ref[b]
        qcol = jnp.broadcast_to(qrow, (page, width)).T
        vn = vn_ref[b]
        s_self = jnp.sum(jnp.where(feat_sel, jnp.broadcast_to(qrow * kn_ref[b], (n_combo, width)), 0.0),
                         axis=1, keepdims=True)
        o8 = None
        for job in range(jobs):
            slot = slot_of(b, job)
            nxt = job + ahead
            if nxt < jobs:
                start(b, nxt, slot_of(b, nxt))
            else:
                @pl.when(b + 1 < n_req)
                def _(nxt=nxt):
                    start(b + 1, nxt - jobs, slot_of(b + 1, nxt - jobs))
            wait(b, job, slot)
            if job < nch:
                def k_page(g, carry2, job=job, slot=slot):
                    prod = buf[slot, g] * qcol
                    s_sc[job * cp + g] = jnp.sum(prod.reshape(n_combo, DIFF_HEAD_DIM, page), axis=1)
                    return carry2

                lax.fori_loop(0, cp, k_page, 0)
                if job == nch - 1:
                    s = s_sc[...]
                    m = jnp.maximum(jnp.max(jnp.max(s, axis=0), axis=1, keepdims=True), s_self)
                    p = jnp.exp(s - m)
                    p_self = jnp.exp(s_self - m)
                    l = jnp.sum(jnp.sum(p, axis=0), axis=1, keepdims=True) + p_self
                    scale = jnp.where(combo % 2 == 0, 1.0, -lam) / l
                    p_sc[...] = p * scale
                    vn8 = jnp.concatenate(
                        [vn[:, (k // 2) * V_HEAD_DIM:(k // 2 + 1) * V_HEAD_DIM] for k in range(n_combo)], axis=0)
                    o8 = (p_self * scale) * vn8
            else:
                jv = job - nch
                pch = p_sc[jv * cp:(jv + 1) * cp].reshape(cp * n_combo, page).astype(BF16)
                x_sc[...] = jnp.where(head_sel, _dot(pch, dilate), 0.0)

                def v_page(g, acc, slot=slot):
                    xg = x_sc[pl.ds(pl.multiple_of(g * n_combo, n_combo), n_combo), :].astype(BF16)
                    return acc + _dot(xg, buf[slot, g].astype(BF16))

                o8 = lax.fori_loop(0, cp, v_page, o8, unroll=4)

        o_ref[b] = jnp.concatenate(
            [_rms(o8[2 * h:2 * h + 1] + o8[2 * h + 1:2 * h + 2], sg) * (1.0 - lam_init)
             for h in range(N_DIFF_HEADS)], axis=1)
        return carry

    lax.fori_loop(0, n_req, request, 0)


def _dec_attn(page_table, q, kn, vn, lams, sg3, cache_kt, cache_v2, layer, lam_init):
    n_req, n_pages = page_table.shape
    page = cache_kt.shape[3]
    cp = PAGES_PER_CHUNK
    n_combo = 2 * N_DIFF_HEADS
    vmem = pl.BlockSpec(memory_space=pltpu.VMEM)
    smem = pl.BlockSpec(memory_space=pltpu.SMEM)
    hbm = pl.BlockSpec(memory_space=pl.ANY)
    return pl.pallas_call(
        functools.partial(_dec_attn_kernel, layer=layer, lam_init=lam_init, n_req=n_req,
                          n_pages=n_pages, page=page),
        in_specs=[smem, vmem, vmem, vmem, vmem, vmem, vmem, vmem, vmem, hbm, hbm],
        out_specs=vmem,
        out_shape=jax.ShapeDtypeStruct((n_req, 1, ATTN_WIDTH), F32),
        scratch_shapes=[pltpu.VMEM((DEC_ATTN_SLOTS, cp, ATTN_WIDTH, page), F32),
                        pltpu.SemaphoreType.DMA((DEC_ATTN_SLOTS,)),
                        pltpu.VMEM((n_pages, n_combo, page), F32),
                        pltpu.VMEM((n_pages, n_combo, page), F32),
                        pltpu.VMEM((cp * n_combo, N_DIFF_HEADS * page), F32)],
        compiler_params=pltpu.CompilerParams(vmem_limit_bytes=VMEM_LIMIT_BYTES),
        name="dec_attn",
    )(page_table, *[a.reshape(n_req, 1, ATTN_WIDTH) for a in (q, kn, vn)],
      *[a[layer] for a in lams], sg3[layer], cache_kt, cache_v2).reshape(n_req, ATTN_WIDTH)


def _dec_mid_kernel(x_ref, at_ref, co_ref, wo_ref, g_ref, wq_ref, x1_ref, qc_ref):
    x1 = (x_ref[...] + _dot(at_ref[...].astype(BF16), wo_ref[0:ATTN_WIDTH, :])
          + _dot(co_ref[...], wo_ref[ATTN_WIDTH:ATTN_WIDTH + CONV_WIDTH, :]))
    x1_ref[...] = x1
    hq = _rms(x1, g_ref[...]).astype(BF16)
    qc_ref[...] = _dot(hq, wq_ref[...]) * (MEM_HEAD_DIM ** -0.5)


def _dec_mid(x, at, co, w_out, norm3, w_cq, layer):
    n, d = x.shape
    full = lambda shape: pl.BlockSpec(shape, lambda i: (0,) * len(shape))
    wspec = pl.BlockSpec((None, d, d), lambda i: (layer, 0, 0))
    return pl.pallas_call(
        _dec_mid_kernel,
        grid=(1,),
        in_specs=[full((n, d)), full((n, ATTN_WIDTH)), full((n, CONV_WIDTH)), wspec,
                  pl.BlockSpec((None, 1, d), lambda i: (layer, 0, 0)), wspec],
        out_specs=[full((n, d)), full((n, d))],
        out_shape=[jax.ShapeDtypeStruct((n, d), F32)] * 2,
        compiler_params=_params(("arbitrary",)),
        name="dec_mid",
    )(x, at, co, w_out, norm3, w_cq)


def _dec_cross_kernel(q_ref, mk_ref, mv_ref, o_ref):
    tile = (mk_ref.shape[0] // SUBLANES, SUBLANES, LANES)
    r = jnp.sum(mk_ref[...].reshape(tile) * q_ref[...][None], axis=2, keepdims=True)
    s = r + pltpu.roll(r, N_MEM_HEADS, 1)
    m = jnp.max(s, axis=0, keepdims=True)
    p = jnp.exp(s - m)
    l = jnp.sum(p, axis=0, keepdims=True)
    o_ref[...] = jnp.sum((p / l) * mv_ref[...].reshape(tile), axis=0)


def _dec_cross(q8, mem_k, mem_v, layer):
    n = q8.shape[0]
    rows = mem_k.shape[2]
    qspec = pl.BlockSpec((None, SUBLANES, LANES), lambda i: (i, 0, 0))
    mspec = pl.BlockSpec((None, None, rows, LANES), lambda i: (layer, i, 0, 0))
    return pl.pallas_call(
        _dec_cross_kernel,
        grid=(n,),
        in_specs=[qspec, mspec, mspec],
        out_specs=qspec,
        out_shape=jax.ShapeDtypeStruct((n, SUBLANES, LANES), F32),
        compiler_params=_params(("arbitrary",)),
        name="dec_cross",
    )(q8, mem_k, mem_v)


def _dec_ffn_kernel(x1_ref, oc_ref, wco_ref, g_ref, wa_ref, wg_ref, cwa_ref, cwg_ref, wd_ref,
                    hist_ref, gf_ref, o_ref, nh_ref, *, final):
    x2 = x1_ref[...] + _dot(oc_ref[...].astype(BF16), wco_ref[...])
    hb = _rms(x2, g_ref[...]).astype(BF16)
    acc = x2
    for c in range(N_FF_CHUNKS):
        ys = []
        for half, (w_ref, cw_ref) in enumerate(((wa_ref, cwa_ref), (wg_ref, cwg_ref))):
            lo = half * D_FF + c * FF_CHUNK
            u = _dot(hb, w_ref[c])
            h0 = hist_ref[:, lo:lo + FF_CHUNK]
            h1 = hist_ref[:, 2 * D_FF + lo:2 * D_FF + lo + FF_CHUNK]
            cw = cw_ref[c]
            ys.append(cw[0:1] * h0 + cw[1:2] * h1 + cw[2:3] * u)
            nh_ref[:, lo:lo + FF_CHUNK] = h1
            nh_ref[:, 2 * D_FF + lo:2 * D_FF + lo + FF_CHUNK] = u
        act = (jax.nn.silu(ys[1]) * ys[0]).astype(BF16)
        acc = acc + _dot(act, wd_ref[c])
    if final:
        acc = _rms(acc, gf_ref[...])
    o_ref[...] = acc


def _dec_ffn(x1, oc, w_co, norm3, wa, wg, cwa, cwg, wd, hist, gf2, layer, final):
    n, d = x1.shape
    nch = N_FF_CHUNKS
    full = lambda shape: pl.BlockSpec(shape, lambda i: (0,) * len(shape))
    return pl.pallas_call(
        functools.partial(_dec_ffn_kernel, final=final),
        grid=(1,),
        in_specs=[full((n, d)), full((n, d)),
                  pl.BlockSpec((None, d, d), lambda i: (layer, 0, 0)),
                  pl.BlockSpec((None, 1, d), lambda i: (layer, 0, 0)),
                  pl.BlockSpec((None, nch, d, FF_CHUNK), lambda i: (layer, 0, 0, 0)),
                  pl.BlockSpec((None, nch, d, FF_CHUNK), lambda i: (layer, 0, 0, 0)),
                  pl.BlockSpec((None, nch, 3, FF_CHUNK), lambda i: (layer, 0, 0, 0)),
                  pl.BlockSpec((None, nch, 3, FF_CHUNK), lambda i: (layer, 0, 0, 0)),
                  pl.BlockSpec((None, nch, FF_CHUNK, d), lambda i: (layer, 0, 0, 0)),
                  pl.BlockSpec((None, n, 4 * D_FF), lambda i: (layer, 0, 0)),
                  full((1, d))],
        out_specs=[full((n, d)), full((n, 4 * D_FF))],
        out_shape=[jax.ShapeDtypeStruct((n, d), F32), jax.ShapeDtypeStruct((n, 4 * D_FF), F32)],
        compiler_params=_params(("arbitrary",)),
        name="dec_ffn",
    )(x1, oc, w_co, norm3, wa, wg, cwa, cwg, wd, hist, gf2)


def kernel(x_prompt, x_sample, mem_prompt, cache_k, cache_v, state_conv, state_ffn_conv,
           cache_mem_k, cache_mem_v, page_table, w_in, w_out, conv_w, lambda_q1, lambda_k1,
           lambda_q2, lambda_k2, subln_g, norm_mix, norm_cross, norm_mem, w_cq, w_ck, w_cv,
           w_co, norm_ffn, w_up, ffn_conv_w, w_down, norm_final):
    depth = w_in.shape[0]
    bp, tp, d = x_prompt.shape
    bs, ts, _ = x_sample.shape
    n_pages = page_table.shape[1]
    page = cache_k.shape[2]
    past_len = n_pages * page
    n_phys = cache_k.shape[1]
    nch = N_FF_CHUNKS

    bf = lambda w: w.astype(BF16)
    w_in_b, w_out_b, w_cq_b, w_ck_b, w_cv_b, w_co_b = map(bf, (w_in, w_out, w_cq, w_ck, w_cv, w_co))
    chunked = lambda w: bf(w).reshape(depth, d, nch, FF_CHUNK).transpose(0, 2, 1, 3)
    wa_b = chunked(w_up[:, :, :D_FF])
    wg_b = chunked(w_up[:, :, D_FF:])
    wd_b = bf(w_down).reshape(depth, nch, FF_CHUNK, d)
    cw_chunked = lambda w: w.reshape(depth, 3, nch, FF_CHUNK).transpose(0, 2, 1, 3)
    cwa = cw_chunked(ffn_conv_w[:, :, :D_FF])
    cwg = cw_chunked(ffn_conv_w[:, :, D_FF:])
    row3 = lambda g: g.reshape(depth, 1, g.shape[-1])
    norm_mix3, norm_cross3, norm_mem3, norm_ffn3, subln3 = map(
        row3, (norm_mix, norm_cross, norm_mem, norm_ffn, subln_g))
    lams = tuple(map(row3, (lambda_q1, lambda_k1, lambda_q2, lambda_k2)))
    gf2 = norm_final.reshape(1, d)
    subln_col = subln_g.reshape(depth, V_HEAD_DIM, 1)

    tables_p = _rope_tables(jnp.arange(tp))
    tables_s = _rope_tables(past_len + jnp.arange(ts))

    cache_k4 = cache_k.transpose(0, 1, 3, 4, 5, 2).reshape(depth, n_phys, ATTN_WIDTH, page)
    cache_v4 = cache_v.reshape(depth, n_phys, N_DIFF_HEADS * page, V_HEAD_DIM)
    halves = MEM_HEAD_DIM // LANES
    mem_rows = lambda a: a.reshape(depth, bs, a.shape[2], N_MEM_HEADS, halves, LANES).transpose(
        0, 1, 2, 4, 3, 5).reshape(depth, bs, a.shape[2] * halves * N_MEM_HEADS, LANES)
    mem_k4 = mem_rows(cache_mem_k)
    mem_v4 = mem_rows(cache_mem_v)
    to_tile = lambda a: a.reshape(bs, N_MEM_HEADS, halves, LANES).transpose(0, 2, 1, 3).reshape(
        bs, halves * N_MEM_HEADS, LANES)
    from_tile = lambda a: a.reshape(bs, halves, N_MEM_HEADS, LANES).transpose(0, 2, 1, 3).reshape(bs, d)
    conv_hist = state_conv.reshape(depth, bs, 2 * CONV_WIDTH)
    ffn_hist = state_ffn_conv.reshape(depth, bs, 4 * D_FF)

    mk, mv, mkb, mvb = _memkv(mem_prompt, norm_mem3, w_ck_b, w_cv_b)

    xp = x_prompt
    xs = x_sample.reshape(bs * ts, d)
    kp_l, vp_l, cp_l, fp_l, ks_l, vs_l, cs_l, fs_l = ([] for _ in range(8))
    for l in range(depth):
        lam_init = 0.8 - 0.6 * math.exp(-0.3 * l)
        final = l == depth - 1
        q1, q2, k, v, kb, vt, co, chist = _mix_in(xp, norm_mix3, w_in_b, conv_w, tables_p, l)
        at = _attn(q1, q2, kb, vt, lams, subln_col, l, lam_init)
        x2 = _mix_out(xp, at, co, w_out_b, norm_cross3, w_cq_b, mkb, mvb, w_co_b, l)
        xp, fhist = _ffn(x2, norm_ffn3, wa_b, wg_b, cwa, cwg, wd_b, gf2, l, final)
        kp_l.append(k)
        vp_l.append(v)
        cp_l.append(chist[:, SUBLANES - 2:, :])
        fp_l.append(fhist[:, :, :, SUBLANES - 2:, :].transpose(0, 3, 2, 1, 4).reshape(bp, 2, 2 * D_FF))
        qd, kd, vd, cod, nconv = _dec_in(xs, norm_mix3, w_in_b, conv_w, conv_hist, tables_s, l)
        atd = _dec_attn(page_table, qd, kd, vd, lams, subln3, cache_k4, cache_v4, l, lam_init)
        x1d, qc = _dec_mid(xs, atd, cod, w_out_b, norm_cross3, w_cq_b, l)
        oc = from_tile(_dec_cross(to_tile(qc), mem_k4, mem_v4, l))
        xs, nffn = _dec_ffn(x1d, oc, w_co_b, norm_ffn3, wa_b, wg_b, cwa, cwg, wd_b, ffn_hist, gf2, l, final)
        ks_l.append(kd)
        vs_l.append(vd)
        cs_l.append(nconv)
        fs_l.append(nffn)

    st = jnp.stack
    return (xp,
            xs.reshape(bs, ts, d),
            st(kp_l).reshape(depth, bp, tp, N_DIFF_HEADS, 2, DIFF_HEAD_DIM),
            st(vp_l).reshape(depth, bp, tp, N_DIFF_HEADS, V_HEAD_DIM),
            st(cp_l),
            st(fp_l),
            mk.reshape(depth, bp, -1, N_MEM_HEADS, MEM_HEAD_DIM),
            mv.reshape(depth, bp, -1, N_MEM_HEADS, MEM_HEAD_DIM),
            st(ks_l).reshape(depth, bs, ts, N_DIFF_HEADS, 2, DIFF_HEAD_DIM),
            st(vs_l).reshape(depth, bs, ts, N_DIFF_HEADS, V_HEAD_DIM),
            st(cs_l).reshape(depth, bs, 2, CONV_WIDTH),
            st(fs_l).reshape(depth, bs, 2, 2 * D_FF))
```

```python
import functools
import math

import jax
import jax.numpy as jnp
from jax import lax
from jax.experimental import pallas as pl
from jax.experimental.pallas import tpu as pltpu

F32 = jnp.float32
BF16 = jnp.bfloat16

D_MODEL = 1024
ATTN_WIDTH = 512
CONV_WIDTH = 512
N_DIFF_HEADS = 4
DIFF_HEAD_DIM = 64
V_HEAD_DIM = 2 * DIFF_HEAD_DIM
ROT_DIM = DIFF_HEAD_DIM // 4
ROPE_THETA = 500000.0
N_MEM_HEADS = 4
MEM_HEAD_DIM = D_MODEL // N_MEM_HEADS
D_FF = 11 * D_MODEL // 4
EPS = 1e-6
NEG_INF = -1e30
LOG2E = math.log2(math.e)

LANES = 128
SUBLANES = 8
VMEM_LIMIT_BYTES = 56 * 1024 * 1024

TM_MIX = 512
TQ = 512
TM_FFN = 1024
FF_CHUNK = 256
N_FF_CHUNKS = D_FF // FF_CHUNK
PAGES_PER_CHUNK = 16
DEC_ATTN_SLOTS = 3


def _dot(a, b):
    return jnp.dot(a, b, preferred_element_type=F32)


def _dot_nt(a, b):
    return lax.dot_general(a, b, (((1,), (1,)), ((), ())), preferred_element_type=F32)


def _rms(x, g):
    ms = jnp.mean(x * x, axis=-1, keepdims=True)
    return x * lax.rsqrt(ms + EPS) * g


def _rope(y, cos, sa, sb):
    outs = []
    for c in range(y.shape[1] // LANES):
        yc = y[:, c * LANES:(c + 1) * LANES]
        outs.append(yc * cos + pltpu.roll(yc, ROT_DIM // 2, 1) * sa
                    + pltpu.roll(yc, LANES - ROT_DIM // 2, 1) * sb)
    return jnp.concatenate(outs, axis=1)


def _rope_tables(pos):
    half = ROT_DIM // 2
    inv = jnp.power(ROPE_THETA, -jnp.arange(half, dtype=F32) * 2.0 / ROT_DIM)
    ang = pos.astype(F32)[:, None] * inv[None, :]
    cos, sin = jnp.cos(ang), jnp.sin(ang)
    n = pos.shape[0]
    ones = jnp.ones((n, DIFF_HEAD_DIM - ROT_DIM), F32)
    zeros = jnp.zeros_like(ones)
    z8 = jnp.zeros((n, half), F32)
    c64 = jnp.concatenate([cos, cos, ones], axis=1)
    sa64 = jnp.concatenate([z8, sin, zeros], axis=1)
    sb64 = jnp.concatenate([-sin, z8, zeros], axis=1)
    tile2 = lambda a: jnp.concatenate([a, a], axis=1)
    return tile2(c64), tile2(sa64), tile2(sb64)


def _lam(lq1, lk1, lq2, lk2, lam_init):
    return (jnp.exp(jnp.sum(lq1 * lk1, axis=-1, keepdims=True))
            - jnp.exp(jnp.sum(lq2 * lk2, axis=-1, keepdims=True)) + lam_init)


def _params(sem=None):
    return pltpu.CompilerParams(dimension_semantics=sem, vmem_limit_bytes=VMEM_LIMIT_BYTES)


def _memkv_kernel(mem_ref, g_ref, wk_ref, wv_ref, mk_ref, mv_ref, mkb_ref, mvb_ref):
    hm = _rms(mem_ref[...], g_ref[...]).astype(BF16)
    mk = _dot(hm, wk_ref[...])
    mv = _dot(hm, wv_ref[...])
    mk_ref[...] = mk
    mv_ref[...] = mv
    mkb_ref[...] = mk.astype(BF16)
    mvb_ref[...] = mv.astype(BF16)


def _memkv(mem, g3, wk, wv):
    depth = wk.shape[0]
    b, n, d = mem.shape
    wspec = pl.BlockSpec((None, d, d), lambda l, i: (l, 0, 0))
    ospec = pl.BlockSpec((None, None, n, d), lambda l, i: (l, i, 0, 0))
    return pl.pallas_call(
        _memkv_kernel,
        grid=(depth, b),
        in_specs=[pl.BlockSpec((None, n, d), lambda l, i: (i, 0, 0)),
                  pl.BlockSpec((None, 1, d), lambda l, i: (l, 0, 0)),
                  wspec, wspec],
        out_specs=[ospec, ospec, ospec, ospec],
        out_shape=[jax.ShapeDtypeStruct((depth, b, n, d), F32)] * 2
        + [jax.ShapeDtypeStruct((depth, b, n, d), BF16)] * 2,
        compiler_params=_params(("arbitrary", "arbitrary")),
        name="memkv",
    )(mem, g3, wk, wv)


def _mix_in_kernel(x_ref, g_ref, w_ref, cw_ref, cos_ref, sa_ref, sb_ref,
                   q1_ref, q2_ref, kt_ref, v_ref, kb_ref, vt_ref, co_ref, hist_ref,
                   cbuf, *, tm):
    t = pl.program_id(1)
    h = _rms(x_ref[...], g_ref[...]).astype(BF16)
    cos, sa, sb = cos_ref[...], sa_ref[...], sb_ref[...]

    def proj(i):
        return _dot(h, w_ref[:, i * ATTN_WIDTH:(i + 1) * ATTN_WIDTH])

    lane = lax.broadcasted_iota(jnp.int32, (tm, ATTN_WIDTH), 1)
    first_map = (lane % V_HEAD_DIM) < DIFF_HEAD_DIM
    q = _rope(proj(0), cos, sa, sb) * (DIFF_HEAD_DIM ** -0.5 * LOG2E)
    q1_ref[...] = jnp.where(first_map, q, 0.0).astype(BF16)
    q2_ref[...] = jnp.where(first_map, 0.0, q).astype(BF16)
    k = _rope(proj(1), cos, sa, sb)
    kt_ref[...] = k.T
    kb_ref[...] = k.astype(BF16)
    v = proj(2)
    for hd in range(N_DIFF_HEADS):
        v_ref[pl.ds(hd, tm, stride=N_DIFF_HEADS), :] = v[:, hd * V_HEAD_DIM:(hd + 1) * V_HEAD_DIM]
    vt_ref[...] = v.T.astype(BF16)

    gate_b = proj(3)
    cu = proj(4) * proj(5)

    @pl.when(t == 0)
    def _():
        cbuf[0:SUBLANES, :] = jnp.zeros((SUBLANES, CONV_WIDTH), F32)

    cbuf[SUBLANES:SUBLANES + tm, :] = cu
    cw = cw_ref[...]
    y = (cw[0:1] * cbuf[SUBLANES - 2:SUBLANES - 2 + tm, :]
         + cw[1:2] * cbuf[SUBLANES - 1:SUBLANES - 1 + tm, :]
         + cw[2:3] * cu)
    co_ref[...] = (gate_b * y).astype(BF16)
    tail = cu[tm - SUBLANES:, :]
    hist_ref[...] = tail
    cbuf[0:SUBLANES, :] = tail


def _mix_in(x, norm3, w_in, conv_w, tables, layer):
    b, t, d = x.shape
    tm = TM_MIX
    cos, sa, sb = tables
    tok = lambda w, dt: jax.ShapeDtypeStruct((b, t, w), dt)
    tspec = pl.BlockSpec((tm, LANES), lambda i, j: (j, 0))
    ospec = pl.BlockSpec((None, tm, ATTN_WIDTH), lambda i, j: (i, j, 0))
    tspec_t = pl.BlockSpec((None, ATTN_WIDTH, tm), lambda i, j: (i, 0, j))
    return pl.pallas_call(
        functools.partial(_mix_in_kernel, tm=tm),
        grid=(b, t // tm),
        in_specs=[pl.BlockSpec((None, tm, d), lambda i, j: (i, j, 0)),
                  pl.BlockSpec((None, 1, d), lambda i, j: (layer, 0, 0)),
                  pl.BlockSpec((None, d, w_in.shape[2]), lambda i, j: (layer, 0, 0)),
                  pl.BlockSpec((None, 3, CONV_WIDTH), lambda i, j: (layer, 0, 0)),
                  tspec, tspec, tspec],
        out_specs=[ospec, ospec, tspec_t,
                   pl.BlockSpec((None, N_DIFF_HEADS * tm, V_HEAD_DIM), lambda i, j: (i, j, 0)),
                   ospec, tspec_t, ospec,
                   pl.BlockSpec((None, SUBLANES, CONV_WIDTH), lambda i, j: (i, 0, 0))],
        out_shape=[tok(ATTN_WIDTH, BF16), tok(ATTN_WIDTH, BF16),
                   jax.ShapeDtypeStruct((b, ATTN_WIDTH, t), F32),
                   jax.ShapeDtypeStruct((b, N_DIFF_HEADS * t, V_HEAD_DIM), F32),
                   tok(ATTN_WIDTH, BF16), jax.ShapeDtypeStruct((b, ATTN_WIDTH, t), BF16), tok(CONV_WIDTH, BF16),
                   jax.ShapeDtypeStruct((b, SUBLANES, CONV_WIDTH), F32)],
        scratch_shapes=[pltpu.VMEM((tm + SUBLANES, CONV_WIDTH), F32)],
        compiler_params=_params(("arbitrary", "arbitrary")),
        name="mix_in",
    )(x, norm3, w_in, conv_w, cos, sa, sb)


def _attn_kernel(qi_ref, kj_ref, q1_ref, q2_ref, k_ref, vt_ref, lq1_ref, lk1_ref, lq2_ref, lk2_ref, sg_ref,
                 o_ref, s_sc, m_sc, l_sc, acc_sc, *, tq, lam_init):
    i = qi_ref[pl.program_id(1)]
    j = kj_ref[pl.program_id(1)]
    ones_rows = 16

    @pl.when(j == 0)
    def _():
        m_sc[...] = jnp.full(m_sc.shape, NEG_INF, F32)
        l_sc[...] = jnp.zeros(l_sc.shape, F32)
        acc_sc[...] = jnp.zeros(acc_sc.shape, F32)

    def step(masked):
        for h in range(N_DIFF_HEADS):
            hs = slice(h * V_HEAD_DIM, (h + 1) * V_HEAD_DIM)
            kh = k_ref[:, hs]
            for c, q_ref in enumerate((q1_ref, q2_ref)):
                s_sc[2 * h + c] = _dot_nt(kh, q_ref[:, hs])
        s = s_sc[...]
        if masked:
            key = lax.broadcasted_iota(jnp.int32, (tq, tq), 0)
            qry = lax.broadcasted_iota(jnp.int32, (tq, tq), 1)
            s = jnp.where((key <= qry)[None], s, NEG_INF)
        m_old = m_sc[...]
        m_new = jnp.maximum(m_old, jnp.max(s, axis=1, keepdims=True))
        alpha = jnp.exp2(m_old - m_new)
        p = jnp.exp2(s - m_new).astype(BF16)
        m_sc[...] = m_new
        ones = jnp.ones((ones_rows, tq), BF16)
        for h in range(N_DIFF_HEADS):
            lhs = jnp.concatenate([vt_ref[h * V_HEAD_DIM:(h + 1) * V_HEAD_DIM, :], ones], axis=0)
            for c in range(2):
                idx = 2 * h + c
                r = _dot(lhs, p[idx])
                acc_sc[idx] = alpha[idx] * acc_sc[idx] + r[0:V_HEAD_DIM]
                l_sc[idx] = alpha[idx] * l_sc[idx] + r[V_HEAD_DIM:V_HEAD_DIM + 1]

    @pl.when(j < i)
    def _():
        step(False)

    @pl.when(j == i)
    def _():
        step(True)
        lam = _lam(lq1_ref[...], lk1_ref[...], lq2_ref[...], lk2_ref[...], lam_init)
        sg = sg_ref[...]
        for h in range(N_DIFF_HEADS):
            o = acc_sc[2 * h] / l_sc[2 * h] - lam * (acc_sc[2 * h + 1] / l_sc[2 * h + 1])
            ms = jnp.mean(o * o, axis=0, keepdims=True)
            o = o * lax.rsqrt(ms + EPS) * sg * (1.0 - lam_init)
            o_ref[:, h * V_HEAD_DIM:(h + 1) * V_HEAD_DIM] = o.T.astype(BF16)


def _attn(q1, q2, kb, vt, lams, sg_col, layer, lam_init):
    b, t, w = q1.shape
    tq = TQ
    nq = t // tq
    n_combo = 2 * N_DIFF_HEADS
    pairs = [(i, j) for i in range(nq) for j in range(i + 1)]
    qi = jnp.asarray([p[0] for p in pairs], jnp.int32)
    kj = jnp.asarray([p[1] for p in pairs], jnp.int32)
    qspec = pl.BlockSpec((None, tq, w), lambda bi, s, qi, kj: (bi, qi[s], 0))
    kspec = pl.BlockSpec((None, tq, w), lambda bi, s, qi, kj: (bi, kj[s], 0))
    vspec = pl.BlockSpec((None, w, tq), lambda bi, s, qi, kj: (bi, 0, kj[s]))
    lspec = pl.BlockSpec((None, 1, DIFF_HEAD_DIM), lambda bi, s, qi, kj: (layer, 0, 0))
    return pl.pallas_call(
        functools.partial(_attn_kernel, tq=tq, lam_init=lam_init),
        grid_spec=pltpu.PrefetchScalarGridSpec(
            num_scalar_prefetch=2,
            grid=(b, len(pairs)),
            in_specs=[qspec, qspec, kspec, vspec, lspec, lspec, lspec, lspec,
                      pl.BlockSpec((None, V_HEAD_DIM, 1), lambda bi, s, qi, kj: (layer, 0, 0))],
            out_specs=pl.BlockSpec((None, tq, w), lambda bi, s, qi, kj: (bi, qi[s], 0)),
            scratch_shapes=[pltpu.VMEM((n_combo, tq, tq), F32),
                            pltpu.VMEM((n_combo, 1, tq), F32),
                            pltpu.VMEM((n_combo, 1, tq), F32),
                            pltpu.VMEM((n_combo, V_HEAD_DIM, tq), F32)]),
        out_shape=jax.ShapeDtypeStruct((b, t, w), BF16),
        compiler_params=_params(("arbitrary", "arbitrary")),
        name="diff_attn",
    )(qi, kj, q1, q2, kb, vt, *lams, sg_col)


def _cross_attend(qc, mk_ref, mv_ref):
    outs = []
    for h in range(N_MEM_HEADS):
        hs = slice(h * MEM_HEAD_DIM, (h + 1) * MEM_HEAD_DIM)
        s = _dot_nt(qc[:, hs], mk_ref[:, hs])
        m = jnp.max(s, axis=1, keepdims=True)
        p = jnp.exp(s - m)
        l = jnp.sum(p, axis=1, keepdims=True)
        outs.append((_dot(p.astype(BF16), mv_ref[:, hs]) / l).astype(BF16))
    return jnp.concatenate(outs, axis=1)


def _mix_out_kernel(x_ref, at_ref, co_ref, wo_ref, g_ref, wq_ref, mk_ref, mv_ref, wco_ref, o_ref):
    x1 = (x_ref[...] + _dot(at_ref[...], wo_ref[0:ATTN_WIDTH, :])
          + _dot(co_ref[...], wo_ref[ATTN_WIDTH:ATTN_WIDTH + CONV_WIDTH, :]))
    hq = _rms(x1, g_ref[...]).astype(BF16)
    qc = (_dot(hq, wq_ref[...]) * (MEM_HEAD_DIM ** -0.5)).astype(BF16)
    o = _cross_attend(qc, mk_ref, mv_ref)
    o_ref[...] = x1 + _dot(o, wco_ref[...])


def _mix_out(x, at, co, w_out, norm3, w_cq, mkb, mvb, w_co, layer):
    b, t, d = x.shape
    tm = TM_MIX
    n_mem = mkb.shape[2]
    wspec = pl.BlockSpec((None, d, d), lambda i, j: (layer, 0, 0))
    hspec = pl.BlockSpec((None, tm, ATTN_WIDTH), lambda i, j: (i, j, 0))
    mspec = pl.BlockSpec((None, None, n_mem, d), lambda i, j: (layer, i, 0, 0))
    xspec = pl.BlockSpec((None, tm, d), lambda i, j: (i, j, 0))
    return pl.pallas_call(
        _mix_out_kernel,
        grid=(b, t // tm),
        in_specs=[xspec, hspec, hspec, wspec,
                  pl.BlockSpec((None, 1, d), lambda i, j: (layer, 0, 0)),
                  wspec, mspec, mspec, wspec],
        out_specs=xspec,
        out_shape=jax.ShapeDtypeStruct((b, t, d), F32),
        compiler_params=_params(("arbitrary", "arbitrary")),
        name="mix_out",
    )(x, at, co, w_out, norm3, w_cq, mkb, mvb, w_co)


def _ffn_kernel(x_ref, g_ref, wa_ref, wg_ref, cwa_ref, cwg_ref, wd_ref, gf_ref,
                o_ref, hist_ref, hb_sc, acc_sc, buf, car_a, car_g, *, tm, final):
    t = pl.program_id(1)
    x = x_ref[...]
    hb_sc[...] = _rms(x, g_ref[...]).astype(BF16)
    acc_sc[...] = x

    @pl.when(t == 0)
    def _():
        car_a[...] = jnp.zeros(car_a.shape, F32)
        car_g[...] = jnp.zeros(car_g.shape, F32)

    def conv(u, stage, car, cw, c):
        stage[0:SUBLANES, :] = car[c]
        stage[SUBLANES:SUBLANES + tm, :] = u
        y = (cw[0:1] * stage[SUBLANES - 2:SUBLANES - 2 + tm, :]
             + cw[1:2] * stage[SUBLANES - 1:SUBLANES - 1 + tm, :]
             + cw[2:3] * u)
        car[c] = u[tm - SUBLANES:, :]
        return y

    def chunk(c, pair_slot):
        hb = hb_sc[...]
        a = conv(_dot(hb, wa_ref[c]), buf.at[2 * pair_slot], car_a, cwa_ref[c], c)
        g = conv(_dot(hb, wg_ref[c]), buf.at[2 * pair_slot + 1], car_g, cwg_ref[c], c)
        act = (jax.nn.silu(g) * a).astype(BF16)
        return _dot(act, wd_ref[c])

    def pair(k, carry):
        acc_sc[...] += chunk(2 * k, 0) + chunk(2 * k + 1, 1)
        return carry

    lax.fori_loop(0, N_FF_CHUNKS // 2, pair, 0)
    for c in range(N_FF_CHUNKS - N_FF_CHUNKS % 2, N_FF_CHUNKS):
        acc_sc[...] += chunk(c, 0)
    hist_ref[:, 0] = car_a[...]
    hist_ref[:, 1] = car_g[...]
    out = acc_sc[...]
    if final:
        out = _rms(out, gf_ref[...])
    o_ref[...] = out


def _ffn(x, norm3, wa, wg, cwa, cwg, wd, gf2, layer, final):
    b, t, d = x.shape
    tm = TM_FFN
    nch = N_FF_CHUNKS
    xspec = pl.BlockSpec((None, tm, d), lambda i, j: (i, j, 0))
    single = pl.Buffered(1)
    return pl.pallas_call(
        functools.partial(_ffn_kernel, tm=tm, final=final),
        grid=(b, t // tm),
        in_specs=[xspec,
                  pl.BlockSpec((None, 1, d), lambda i, j: (layer, 0, 0)),
                  pl.BlockSpec((None, nch, d, FF_CHUNK), lambda i, j: (layer, 0, 0, 0), pipeline_mode=single),
                  pl.BlockSpec((None, nch, d, FF_CHUNK), lambda i, j: (layer, 0, 0, 0), pipeline_mode=single),
                  pl.BlockSpec((None, nch, 3, FF_CHUNK), lambda i, j: (layer, 0, 0, 0)),
                  pl.BlockSpec((None, nch, 3, FF_CHUNK), lambda i, j: (layer, 0, 0, 0)),
                  pl.BlockSpec((None, nch, FF_CHUNK, d), lambda i, j: (layer, 0, 0, 0), pipeline_mode=single),
                  pl.BlockSpec((1, d), lambda i, j: (0, 0))],
        out_specs=[xspec,
                   pl.BlockSpec((None, nch, 2, SUBLANES, FF_CHUNK), lambda i, j: (i, 0, 0, 0, 0))],
        out_shape=[jax.ShapeDtypeStruct((b, t, d), F32),
                   jax.ShapeDtypeStruct((b, nch, 2, SUBLANES, FF_CHUNK), F32)],
        scratch_shapes=[pltpu.VMEM((tm, d), BF16),
                        pltpu.VMEM((tm, d), F32),
                        pltpu.VMEM((4, tm + SUBLANES, FF_CHUNK), F32),
                        pltpu.VMEM((nch, SUBLANES, FF_CHUNK), F32),
                        pltpu.VMEM((nch, SUBLANES, FF_CHUNK), F32)],
        compiler_params=_params(("arbitrary", "arbitrary")),
        name="ffn",
    )(x, norm3, wa, wg, cwa, cwg, wd, gf2)


def _dec_in_kernel(x_ref, g_ref, w_ref, cw_ref, hist_ref, cos_ref, sa_ref, sb_ref,
                   q_ref, k_ref, v_ref, co_ref, nh_ref):
    h = _rms(x_ref[...], g_ref[...]).astype(BF16)
    cos, sa, sb = cos_ref[...], sa_ref[...], sb_ref[...]

    def proj(i):
        return _dot(h, w_ref[:, i * ATTN_WIDTH:(i + 1) * ATTN_WIDTH])

    q_ref[...] = _rope(proj(0), cos, sa, sb) * (DIFF_HEAD_DIM ** -0.5)
    k_ref[...] = _rope(proj(1), cos, sa, sb)
    v_ref[...] = proj(2)
    gate_b = proj(3)
    cu = proj(4) * proj(5)
    h0 = hist_ref[:, 0:CONV_WIDTH]
    h1 = hist_ref[:, CONV_WIDTH:2 * CONV_WIDTH]
    cw = cw_ref[...]
    y = cw[0:1] * h0 + cw[1:2] * h1 + cw[2:3] * cu
    co_ref[...] = (gate_b * y).astype(BF16)
    nh_ref[:, 0:CONV_WIDTH] = h1
    nh_ref[:, CONV_WIDTH:2 * CONV_WIDTH] = cu


def _dec_in(x, norm3, w_in, conv_w, hist, tables, layer):
    n, d = x.shape
    cos, sa, sb = tables
    full = lambda shape: pl.BlockSpec(shape, lambda i: (0,) * len(shape))
    ospec = full((n, ATTN_WIDTH))
    return pl.pallas_call(
        _dec_in_kernel,
        grid=(1,),
        in_specs=[full((n, d)),
                  pl.BlockSpec((None, 1, d), lambda i: (layer, 0, 0)),
                  pl.BlockSpec((None, d, w_in.shape[2]), lambda i: (layer, 0, 0)),
                  pl.BlockSpec((None, 3, CONV_WIDTH), lambda i: (layer, 0, 0)),
                  pl.BlockSpec((None, n, 2 * CONV_WIDTH), lambda i: (layer, 0, 0)),
                  full((1, LANES)), full((1, LANES)), full((1, LANES))],
        out_specs=[ospec, ospec, ospec, ospec, full((n, 2 * CONV_WIDTH))],
        out_shape=[jax.ShapeDtypeStruct((n, ATTN_WIDTH), F32)] * 3
        + [jax.ShapeDtypeStruct((n, CONV_WIDTH), BF16),
           jax.ShapeDtypeStruct((n, 2 * CONV_WIDTH), F32)],
        compiler_params=_params(("arbitrary",)),
        name="dec_in",
    )(x, norm3, w_in, conv_w, hist, cos, sa, sb)


def _group_select(n_rows, width, group):
    r = lax.broadcasted_iota(jnp.int32, (n_rows, width), 0)
    c = lax.broadcasted_iota(jnp.int32, (n_rows, width), 1)
    return r == c // group


def _dec_attn_kernel(pt_ref, q_ref, kn_ref, vn_ref, lq1_ref, lk1_ref, lq2_ref, lk2_ref, sg_ref,
                     kt_hbm, v_hbm, o_ref, buf, sem, s_sc, p_sc, x_sc,
                     *, layer, lam_init, n_req, n_pages, page):
    cp = PAGES_PER_CHUNK
    nch = n_pages // cp
    width = ATTN_WIDTH
    n_combo = 2 * N_DIFF_HEADS
    jobs = 2 * nch

    def copies(b, job, slot):
        src = kt_hbm if job < nch else v_hbm
        c = job % nch
        return [pltpu.make_async_copy(src.at[layer, pt_ref[b, c * cp + g]], buf.at[slot, g], sem.at[slot])
                for g in range(cp)]

    def start(b, job, slot):
        for cpy in copies(b, job, slot):
            cpy.start()

    def wait(b, job, slot):
        for cpy in copies(b, job, slot):
            cpy.wait()

    lam = _lam(lq1_ref[...], lk1_ref[...], lq2_ref[...], lk2_ref[...], lam_init)
    sg = sg_ref[...]
    combo = lax.broadcasted_iota(jnp.int32, (n_combo, 1), 0)
    feat_sel = _group_select(n_combo, width, DIFF_HEAD_DIM)
    r = lax.broadcasted_iota(jnp.int32, (page, N_DIFF_HEADS * page), 0)
    c = lax.broadcasted_iota(jnp.int32, (page, N_DIFF_HEADS * page), 1)
    dilate = jnp.where(c // N_DIFF_HEADS == r, 1.0, 0.0).astype(BF16)
    rk = lax.broadcasted_iota(jnp.int32, (cp * n_combo, N_DIFF_HEADS * page), 0) % n_combo
    ch = lax.broadcasted_iota(jnp.int32, (cp * n_combo, N_DIFF_HEADS * page), 1) % N_DIFF_HEADS
    head_sel = rk // 2 == ch

    ahead = DEC_ATTN_SLOTS - 1

    def slot_of(b, job):
        return lax.rem(b * jobs + job, DEC_ATTN_SLOTS)

    for first in range(ahead):
        start(0, first, first)

    def request(b, carry):
        qrow = q_ref[b]
        qcol = jnp.broadcast_to(qrow, (page, width)).T
        vn = vn_ref[b]
        s_self = jnp.sum(jnp.where(feat_sel, jnp.broadcast_to(qrow * kn_ref[b], (n_combo, width)), 0.0),
                         axis=1, keepdims=True)
        o8 = None
        for job in range(jobs):
            slot = slot_of(b, job)
            nxt = job + ahead
            if nxt < jobs:
                start(b, nxt, slot_of(b, nxt))
            else:
                @pl.when(b + 1 < n_req)
                def _(nxt=nxt):
                    start(b + 1, nxt - jobs, slot_of(b + 1, nxt - jobs))
            wait(b, job, slot)
            if job < nch:
                def k_page(g, carry2, job=job, slot=slot):
                    prod = buf[slot, g] * qcol
                    s_sc[job * cp + g] = jnp.sum(prod.reshape(n_combo, DIFF_HEAD_DIM, page), axis=1)
                    return carry2

                lax.fori_loop(0, cp, k_page, 0)
                if job == nch - 1:
                    s = s_sc[...]
                    m = jnp.maximum(jnp.max(jnp.max(s, axis=0), axis=1, keepdims=True), s_self)
                    p = jnp.exp(s - m)
                    p_self = jnp.exp(s_self - m)
                    l = jnp.sum(jnp.sum(p, axis=0), axis=1, keepdims=True) + p_self
                    scale = jnp.where(combo % 2 == 0, 1.0, -lam) / l
                    p_sc[...] = p * scale
                    vn8 = jnp.concatenate(
                        [vn[:, (k // 2) * V_HEAD_DIM:(k // 2 + 1) * V_HEAD_DIM] for k in range(n_combo)], axis=0)
                    o8 = (p_self * scale) * vn8
            else:
                jv = job - nch
                pch = p_sc[jv * cp:(jv + 1) * cp].reshape(cp * n_combo, page).astype(BF16)
                x_sc[...] = jnp.where(head_sel, _dot(pch, dilate), 0.0)

                def v_page(g, acc, slot=slot):
                    xg = x_sc[pl.ds(pl.multiple_of(g * n_combo, n_combo), n_combo), :].astype(BF16)
                    return acc + _dot(xg, buf[slot, g].astype(BF16))

                o8 = lax.fori_loop(0, cp, v_page, o8, unroll=4)

        o_ref[b] = jnp.concatenate(
            [_rms(o8[2 * h:2 * h + 1] + o8[2 * h + 1:2 * h + 2], sg) * (1.0 - lam_init)
             for h in range(N_DIFF_HEADS)], axis=1)
        return carry

    lax.fori_loop(0, n_req, request, 0)


def _dec_attn(page_table, q, kn, vn, lams, sg3, cache_kt, cache_v2, layer, lam_init):
    n_req, n_pages = page_table.shape
    page = cache_kt.shape[3]
    cp = PAGES_PER_CHUNK
    n_combo = 2 * N_DIFF_HEADS
    vmem = pl.BlockSpec(memory_space=pltpu.VMEM)
    smem = pl.BlockSpec(memory_space=pltpu.SMEM)
    hbm = pl.BlockSpec(memory_space=pl.ANY)
    return pl.pallas_call(
        functools.partial(_dec_attn_kernel, layer=layer, lam_init=lam_init, n_req=n_req,
                          n_pages=n_pages, page=page),
        in_specs=[smem, vmem, vmem, vmem, vmem, vmem, vmem, vmem, vmem, hbm, hbm],
        out_specs=vmem,
        out_shape=jax.ShapeDtypeStruct((n_req, 1, ATTN_WIDTH), F32),
        scratch_shapes=[pltpu.VMEM((DEC_ATTN_SLOTS, cp, ATTN_WIDTH, page), F32),
                        pltpu.SemaphoreType.DMA((DEC_ATTN_SLOTS,)),
                        pltpu.VMEM((n_pages, n_combo, page), F32),
                        pltpu.VMEM((n_pages, n_combo, page), F32),
                        pltpu.VMEM((cp * n_combo, N_DIFF_HEADS * page), F32)],
        compiler_params=pltpu.CompilerParams(vmem_limit_bytes=VMEM_LIMIT_BYTES),
        name="dec_attn",
    )(page_table, *[a.reshape(n_req, 1, ATTN_WIDTH) for a in (q, kn, vn)],
      *[a[layer] for a in lams], sg3[layer], cache_kt, cache_v2).reshape(n_req, ATTN_WIDTH)


def _dec_mid_kernel(x_ref, at_ref, co_ref, wo_ref, g_ref, wq_ref, x1_ref, qc_ref):
    x1 = (x_ref[...] + _dot(at_ref[...].astype(BF16), wo_ref[0:ATTN_WIDTH, :])
          + _dot(co_ref[...], wo_ref[ATTN_WIDTH:ATTN_WIDTH + CONV_WIDTH, :]))
    x1_ref[...] = x1
    hq = _rms(x1, g_ref[...]).astype(BF16)
    qc_ref[...] = _dot(hq, wq_ref[...]) * (MEM_HEAD_DIM ** -0.5)


def _dec_mid(x, at, co, w_out, norm3, w_cq, layer):
    n, d = x.shape
    full = lambda shape: pl.BlockSpec(shape, lambda i: (0,) * len(shape))
    wspec = pl.BlockSpec((None, d, d), lambda i: (layer, 0, 0))
    return pl.pallas_call(
        _dec_mid_kernel,
        grid=(1,),
        in_specs=[full((n, d)), full((n, ATTN_WIDTH)), full((n, CONV_WIDTH)), wspec,
                  pl.BlockSpec((None, 1, d), lambda i: (layer, 0, 0)), wspec],
        out_specs=[full((n, d)), full((n, d))],
        out_shape=[jax.ShapeDtypeStruct((n, d), F32)] * 2,
        compiler_params=_params(("arbitrary",)),
        name="dec_mid",
    )(x, at, co, w_out, norm3, w_cq)


def _dec_cross_kernel(q_ref, mk_ref, mv_ref, o_ref):
    tile = (mk_ref.shape[0] // SUBLANES, SUBLANES, LANES)
    r = jnp.sum(mk_ref[...].reshape(tile) * q_ref[...][None], axis=2, keepdims=True)
    s = r + pltpu.roll(r, N_MEM_HEADS, 1)
    m = jnp.max(s, axis=0, keepdims=True)
    p = jnp.exp(s - m)
    l = jnp.sum(p, axis=0, keepdims=True)
    o_ref[...] = jnp.sum((p / l) * mv_ref[...].reshape(tile), axis=0)


def _dec_cross(q8, mem_k, mem_v, layer):
    n = q8.shape[0]
    rows = mem_k.shape[2]
    qspec = pl.BlockSpec((None, SUBLANES, LANES), lambda i: (i, 0, 0))
    mspec = pl.BlockSpec((None, None, rows, LANES), lambda i: (layer, i, 0, 0))
    return pl.pallas_call(
        _dec_cross_kernel,
        grid=(n,),
        in_specs=[qspec, mspec, mspec],
        out_specs=qspec,
        out_shape=jax.ShapeDtypeStruct((n, SUBLANES, LANES), F32),
        compiler_params=_params(("arbitrary",)),
        name="dec_cross",
    )(q8, mem_k, mem_v)


def _dec_ffn_kernel(x1_ref, oc_ref, wco_ref, g_ref, wa_ref, wg_ref, cwa_ref, cwg_ref, wd_ref,
                    hist_ref, gf_ref, o_ref, nh_ref, *, final):
    x2 = x1_ref[...] + _dot(oc_ref[...].astype(BF16), wco_ref[...])
    hb = _rms(x2, g_ref[...]).astype(BF16)
    acc = x2
    for c in range(N_FF_CHUNKS):
        ys = []
        for half, (w_ref, cw_ref) in enumerate(((wa_ref, cwa_ref), (wg_ref, cwg_ref))):
            lo = half * D_FF + c * FF_CHUNK
            u = _dot(hb, w_ref[c])
            h0 = hist_ref[:, lo:lo + FF_CHUNK]
            h1 = hist_ref[:, 2 * D_FF + lo:2 * D_FF + lo + FF_CHUNK]
            cw = cw_ref[c]
            ys.append(cw[0:1] * h0 + cw[1:2] * h1 + cw[2:3] * u)
            nh_ref[:, lo:lo + FF_CHUNK] = h1
            nh_ref[:, 2 * D_FF + lo:2 * D_FF + lo + FF_CHUNK] = u
        act = (jax.nn.silu(ys[1]) * ys[0]).astype(BF16)
        acc = acc + _dot(act, wd_ref[c])
    if final:
        acc = _rms(acc, gf_ref[...])
    o_ref[...] = acc


def _dec_ffn(x1, oc, w_co, norm3, wa, wg, cwa, cwg, wd, hist, gf2, layer, final):
    n, d = x1.shape
    nch = N_FF_CHUNKS
    full = lambda shape: pl.BlockSpec(shape, lambda i: (0,) * len(shape))
    return pl.pallas_call(
        functools.partial(_dec_ffn_kernel, final=final),
        grid=(1,),
        in_specs=[full((n, d)), full((n, d)),
                  pl.BlockSpec((None, d, d), lambda i: (layer, 0, 0)),
                  pl.BlockSpec((None, 1, d), lambda i: (layer, 0, 0)),
                  pl.BlockSpec((None, nch, d, FF_CHUNK), lambda i: (layer, 0, 0, 0)),
                  pl.BlockSpec((None, nch, d, FF_CHUNK), lambda i: (layer, 0, 0, 0)),
                  pl.BlockSpec((None, nch, 3, FF_CHUNK), lambda i: (layer, 0, 0, 0)),
                  pl.BlockSpec((None, nch, 3, FF_CHUNK), lambda i: (layer, 0, 0, 0)),
                  pl.BlockSpec((None, nch, FF_CHUNK, d), lambda i: (layer, 0, 0, 0)),
                  pl.BlockSpec((None, n, 4 * D_FF), lambda i: (layer, 0, 0)),
                  full((1, d))],
        out_specs=[full((n, d)), full((n, 4 * D_FF))],
        out_shape=[jax.ShapeDtypeStruct((n, d), F32), jax.ShapeDtypeStruct((n, 4 * D_FF), F32)],
        compiler_params=_params(("arbitrary",)),
        name="dec_ffn",
    )(x1, oc, w_co, norm3, wa, wg, cwa, cwg, wd, hist, gf2)


def kernel(x_prompt, x_sample, mem_prompt, cache_k, cache_v, state_conv, state_ffn_conv,
           cache_mem_k, cache_mem_v, page_table, w_in, w_out, conv_w, lambda_q1, lambda_k1,
           lambda_q2, lambda_k2, subln_g, norm_mix, norm_cross, norm_mem, w_cq, w_ck, w_cv,
           w_co, norm_ffn, w_up, ffn_conv_w, w_down, norm_final):
    depth = w_in.shape[0]
    bp, tp, d = x_prompt.shape
    bs, ts, _ = x_sample.shape
    n_pages = page_table.shape[1]
    page = cache_k.shape[2]
    past_len = n_pages * page
    n_phys = cache_k.shape[1]
    nch = N_FF_CHUNKS

    bf = lambda w: w.astype(BF16)
    w_in_b, w_out_b, w_cq_b, w_ck_b, w_cv_b, w_co_b = map(bf, (w_in, w_out, w_cq, w_ck, w_cv, w_co))
    chunked = lambda w: bf(w).reshape(depth, d, nch, FF_CHUNK).transpose(0, 2, 1, 3)
    wa_b = chunked(w_up[:, :, :D_FF])
    wg_b = chunked(w_up[:, :, D_FF:])
    wd_b = bf(w_down).reshape(depth, nch, FF_CHUNK, d)
    cw_chunked = lambda w: w.reshape(depth, 3, nch, FF_CHUNK).transpose(0, 2, 1, 3)
    cwa = cw_chunked(ffn_conv_w[:, :, :D_FF])
    cwg = cw_chunked(ffn_conv_w[:, :, D_FF:])
    row3 = lambda g: g.reshape(depth, 1, g.shape[-1])
    norm_mix3, norm_cross3, norm_mem3, norm_ffn3, subln3 = map(
        row3, (norm_mix, norm_cross, norm_mem, norm_ffn, subln_g))
    lams = tuple(map(row3, (lambda_q1, lambda_k1, lambda_q2, lambda_k2)))
    gf2 = norm_final.reshape(1, d)
    subln_col = subln_g.reshape(depth, V_HEAD_DIM, 1)

    tables_p = _rope_tables(jnp.arange(tp))
    tables_s = _rope_tables(past_len + jnp.arange(ts))

    cache_k4 = cache_k.transpose(0, 1, 3, 4, 5, 2).reshape(depth, n_phys, ATTN_WIDTH, page)
    cache_v4 = cache_v.reshape(depth, n_phys, N_DIFF_HEADS * page, V_HEAD_DIM)
    halves = MEM_HEAD_DIM // LANES
    mem_rows = lambda a: a.reshape(depth, bs, a.shape[2], N_MEM_HEADS, halves, LANES).transpose(
        0, 1, 2, 4, 3, 5).reshape(depth, bs, a.shape[2] * halves * N_MEM_HEADS, LANES)
    mem_k4 = mem_rows(cache_mem_k)
    mem_v4 = mem_rows(cache_mem_v)
    to_tile = lambda a: a.reshape(bs, N_MEM_HEADS, halves, LANES).transpose(0, 2, 1, 3).reshape(
        bs, halves * N_MEM_HEADS, LANES)
    from_tile = lambda a: a.reshape(bs, halves, N_MEM_HEADS, LANES).transpose(0, 2, 1, 3).reshape(bs, d)
    conv_hist = state_conv.reshape(depth, bs, 2 * CONV_WIDTH)
    ffn_hist = state_ffn_conv.reshape(depth, bs, 4 * D_FF)

    mk, mv, mkb, mvb = _memkv(mem_prompt, norm_mem3, w_ck_b, w_cv_b)

    xp = x_prompt
    xs = x_sample.reshape(bs * ts, d)
    kp_l, vp_l, cp_l, fp_l, ks_l, vs_l, cs_l, fs_l = ([] for _ in range(8))
    for l in range(depth):
        lam_init = 0.8 - 0.6 * math.exp(-0.3 * l)
        final = l == depth - 1
        q1, q2, kt, v4, kb, vt, co, chist = _mix_in(xp, norm_mix3, w_in_b, conv_w, tables_p, l)
        at = _attn(q1, q2, kb, vt, lams, subln_col, l, lam_init)
        x2 = _mix_out(xp, at, co, w_out_b, norm_cross3, w_cq_b, mkb, mvb, w_co_b, l)
        xp, fhist = _ffn(x2, norm_ffn3, wa_b, wg_b, cwa, cwg, wd_b, gf2, l, final)
        kp_l.append(kt)
        vp_l.append(v4)
        cp_l.append(chist[:, SUBLANES - 2:, :])
        fp_l.append(fhist[:, :, :, SUBLANES - 2:, :].transpose(0, 3, 2, 1, 4).reshape(bp, 2, 2 * D_FF))
        qd, kd, vd, cod, nconv = _dec_in(xs, norm_mix3, w_in_b, conv_w, conv_hist, tables_s, l)
        atd = _dec_attn(page_table, qd, kd, vd, lams, subln3, cache_k4, cache_v4, l, lam_init)
        x1d, qc = _dec_mid(xs, atd, cod, w_out_b, norm_cross3, w_cq_b, l)
        oc = from_tile(_dec_cross(to_tile(qc), mem_k4, mem_v4, l))
        xs, nffn = _dec_ffn(x1d, oc, w_co_b, norm_ffn3, wa_b, wg_b, cwa, cwg, wd_b, ffn_hist, gf2, l, final)
        ks_l.append(kd)
        vs_l.append(vd)
        cs_l.append(nconv)
        fs_l.append(nffn)

    st = jnp.stack
    return (xp,
            xs.reshape(bs, ts, d),
            st(kp_l).reshape(depth, bp, N_DIFF_HEADS, 2, DIFF_HEAD_DIM, tp).transpose(0, 1, 5, 2, 3, 4),
            st(vp_l).reshape(depth, bp, tp, N_DIFF_HEADS, V_HEAD_DIM),
            st(cp_l),
            st(fp_l),
            mk.reshape(depth, bp, -1, N_MEM_HEADS, MEM_HEAD_DIM),
            mv.reshape(depth, bp, -1, N_MEM_HEADS, MEM_HEAD_DIM),
            st(ks_l).reshape(depth, bs, ts, N_DIFF_HEADS, 2, DIFF_HEAD_DIM),
            st(vs_l).reshape(depth, bs, ts, N_DIFF_HEADS, V_HEAD_DIM),
            st(cs_l).reshape(depth, bs, 2, CONV_WIDTH),
            st(fs_l).reshape(depth, bs, 2, 2 * D_FF))
```

```python
import functools
import math

import jax
import jax.numpy as jnp
from jax import lax
from jax.experimental import pallas as pl
from jax.experimental.pallas import tpu as pltpu

F32 = jnp.float32
BF16 = jnp.bfloat16

D_MODEL = 1024
ATTN_WIDTH = 512
CONV_WIDTH = 512
N_DIFF_HEADS = 4
DIFF_HEAD_DIM = 64
V_HEAD_DIM = 2 * DIFF_HEAD_DIM
ROT_DIM = DIFF_HEAD_DIM // 4
ROPE_THETA = 500000.0
N_MEM_HEADS = 4
MEM_HEAD_DIM = D_MODEL // N_MEM_HEADS
D_FF = 11 * D_MODEL // 4
EPS = 1e-6
NEG_INF = -1e30
LOG2E = math.log2(math.e)

LANES = 128
SUBLANES = 8
VMEM_LIMIT_BYTES = 56 * 1024 * 1024

TM_MIX = 1024
TQ = 512
TM_FFN = 1024
FF_CHUNK = 256
N_FF_CHUNKS = D_FF // FF_CHUNK
PAGES_PER_CHUNK = 16
DEC_ATTN_SLOTS = 3


def _dot(a, b):
    return jnp.dot(a, b, preferred_element_type=F32)


def _dot_nt(a, b):
    return lax.dot_general(a, b, (((1,), (1,)), ((), ())), preferred_element_type=F32)


def _rms(x, g):
    ms = jnp.mean(x * x, axis=-1, keepdims=True)
    return x * lax.rsqrt(ms + EPS) * g


def _rope(y, cos, sa, sb):
    outs = []
    for c in range(y.shape[1] // LANES):
        yc = y[:, c * LANES:(c + 1) * LANES]
        outs.append(yc * cos + pltpu.roll(yc, ROT_DIM // 2, 1) * sa
                    + pltpu.roll(yc, LANES - ROT_DIM // 2, 1) * sb)
    return jnp.concatenate(outs, axis=1)


def _rope_tables(pos):
    half = ROT_DIM // 2
    inv = jnp.power(ROPE_THETA, -jnp.arange(half, dtype=F32) * 2.0 / ROT_DIM)
    ang = pos.astype(F32)[:, None] * inv[None, :]
    cos, sin = jnp.cos(ang), jnp.sin(ang)
    n = pos.shape[0]
    ones = jnp.ones((n, DIFF_HEAD_DIM - ROT_DIM), F32)
    zeros = jnp.zeros_like(ones)
    z8 = jnp.zeros((n, half), F32)
    c64 = jnp.concatenate([cos, cos, ones], axis=1)
    sa64 = jnp.concatenate([z8, sin, zeros], axis=1)
    sb64 = jnp.concatenate([-sin, z8, zeros], axis=1)
    tile2 = lambda a: jnp.concatenate([a, a], axis=1)
    return tile2(c64), tile2(sa64), tile2(sb64)


def _lam(lq1, lk1, lq2, lk2, lam_init):
    return (jnp.exp(jnp.sum(lq1 * lk1, axis=-1, keepdims=True))
            - jnp.exp(jnp.sum(lq2 * lk2, axis=-1, keepdims=True)) + lam_init)


def _params(sem=None):
    return pltpu.CompilerParams(dimension_semantics=sem, vmem_limit_bytes=VMEM_LIMIT_BYTES)


def _cast_weights(first, pairs):
    @pl.when(first)
    def _():
        for src, dst in pairs:
            dst[...] = src[...].astype(BF16)


def _memkv_kernel(mem_ref, g_ref, wk_ref, wv_ref, mk_ref, mv_ref, mkb_ref, mvb_ref, wkb, wvb):
    _cast_weights(pl.program_id(1) == 0, ((wk_ref, wkb), (wv_ref, wvb)))
    hm = _rms(mem_ref[...], g_ref[...]).astype(BF16)
    mk = _dot(hm, wkb[...])
    mv = _dot(hm, wvb[...])
    mk_ref[...] = mk
    mv_ref[...] = mv
    mkb_ref[...] = mk.astype(BF16)
    mvb_ref[...] = mv.astype(BF16)


def _memkv(mem, g3, wk, wv):
    depth = wk.shape[0]
    b, n, d = mem.shape
    wspec = pl.BlockSpec((None, d, d), lambda l, i: (l, 0, 0))
    ospec = pl.BlockSpec((None, None, n, d), lambda l, i: (l, i, 0, 0))
    return pl.pallas_call(
        _memkv_kernel,
        grid=(depth, b),
        in_specs=[pl.BlockSpec((None, n, d), lambda l, i: (i, 0, 0)),
                  pl.BlockSpec((None, 1, d), lambda l, i: (l, 0, 0)),
                  wspec, wspec],
        out_specs=[ospec, ospec, ospec, ospec],
        out_shape=[jax.ShapeDtypeStruct((depth, b, n, d), F32)] * 2
        + [jax.ShapeDtypeStruct((depth, b, n, d), BF16)] * 2,
        scratch_shapes=[pltpu.VMEM((d, d), BF16)] * 2,
        compiler_params=_params(("arbitrary", "arbitrary")),
        name="memkv",
    )(mem, g3, wk, wv)


def _mix_in_kernel(x_ref, g_ref, w_ref, cw_ref, cos_ref, sa_ref, sb_ref,
                   q1_ref, q2_ref, kt_ref, v_ref, kb_ref, vt_ref, co_ref, hist_ref,
                   cbuf, wb, *, tm):
    t = pl.program_id(1)
    _cast_weights((pl.program_id(0) == 0) & (t == 0), ((w_ref, wb),))
    h = _rms(x_ref[...], g_ref[...]).astype(BF16)
    cos, sa, sb = cos_ref[...], sa_ref[...], sb_ref[...]

    def proj(i):
        return _dot(h, wb[:, i * ATTN_WIDTH:(i + 1) * ATTN_WIDTH])

    lane = lax.broadcasted_iota(jnp.int32, (tm, ATTN_WIDTH), 1)
    first_map = (lane % V_HEAD_DIM) < DIFF_HEAD_DIM
    q = _rope(proj(0), cos, sa, sb) * (DIFF_HEAD_DIM ** -0.5 * LOG2E)
    q1_ref[...] = jnp.where(first_map, q, 0.0).astype(BF16)
    q2_ref[...] = jnp.where(first_map, 0.0, q).astype(BF16)
    k = _rope(proj(1), cos, sa, sb)
    kt_ref[...] = k.T
    kb_ref[...] = k.astype(BF16)
    v = proj(2)
    for hd in range(N_DIFF_HEADS):
        v_ref[pl.ds(hd, tm, stride=N_DIFF_HEADS), :] = v[:, hd * V_HEAD_DIM:(hd + 1) * V_HEAD_DIM]
    vt_ref[...] = v.T.astype(BF16)

    gate_b = proj(3)
    cu = proj(4) * proj(5)

    @pl.when(t == 0)
    def _():
        cbuf[0:SUBLANES, :] = jnp.zeros((SUBLANES, CONV_WIDTH), F32)

    cbuf[SUBLANES:SUBLANES + tm, :] = cu
    cw = cw_ref[...]
    y = (cw[0:1] * cbuf[SUBLANES - 2:SUBLANES - 2 + tm, :]
         + cw[1:2] * cbuf[SUBLANES - 1:SUBLANES - 1 + tm, :]
         + cw[2:3] * cu)
    co_ref[...] = (gate_b * y).astype(BF16)
    tail = cu[tm - SUBLANES:, :]
    hist_ref[...] = tail
    cbuf[0:SUBLANES, :] = tail


def _mix_in(x, norm3, w_in, conv_w, tables, layer):
    b, t, d = x.shape
    tm = TM_MIX
    cos, sa, sb = tables
    tok = lambda w, dt: jax.ShapeDtypeStruct((b, t, w), dt)
    tspec = pl.BlockSpec((tm, LANES), lambda i, j: (j, 0))
    ospec = pl.BlockSpec((None, tm, ATTN_WIDTH), lambda i, j: (i, j, 0))
    tspec_t = pl.BlockSpec((None, ATTN_WIDTH, tm), lambda i, j: (i, 0, j))
    return pl.pallas_call(
        functools.partial(_mix_in_kernel, tm=tm),
        grid=(b, t // tm),
        in_specs=[pl.BlockSpec((None, tm, d), lambda i, j: (i, j, 0)),
                  pl.BlockSpec((None, 1, d), lambda i, j: (layer, 0, 0)),
                  pl.BlockSpec((None, d, w_in.shape[2]), lambda i, j: (layer, 0, 0), pipeline_mode=pl.Buffered(1)),
                  pl.BlockSpec((None, 3, CONV_WIDTH), lambda i, j: (layer, 0, 0)),
                  tspec, tspec, tspec],
        out_specs=[ospec, ospec, tspec_t,
                   pl.BlockSpec((None, N_DIFF_HEADS * tm, V_HEAD_DIM), lambda i, j: (i, j, 0)),
                   ospec, tspec_t, ospec,
                   pl.BlockSpec((None, SUBLANES, CONV_WIDTH), lambda i, j: (i, 0, 0))],
        out_shape=[tok(ATTN_WIDTH, BF16), tok(ATTN_WIDTH, BF16),
                   jax.ShapeDtypeStruct((b, ATTN_WIDTH, t), F32),
                   jax.ShapeDtypeStruct((b, N_DIFF_HEADS * t, V_HEAD_DIM), F32),
                   tok(ATTN_WIDTH, BF16), jax.ShapeDtypeStruct((b, ATTN_WIDTH, t), BF16), tok(CONV_WIDTH, BF16),
                   jax.ShapeDtypeStruct((b, SUBLANES, CONV_WIDTH), F32)],
        scratch_shapes=[pltpu.VMEM((tm + SUBLANES, CONV_WIDTH), F32),
                        pltpu.VMEM((d, w_in.shape[2]), BF16)],
        compiler_params=_params(("arbitrary", "arbitrary")),
        name="mix_in",
    )(x, norm3, w_in, conv_w, cos, sa, sb)


def _attn_kernel(qi_ref, kj_ref, q1_ref, q2_ref, k_ref, vt_ref, lq1_ref, lk1_ref, lq2_ref, lk2_ref, sg_ref,
                 o_ref, s_sc, m_sc, l_sc, acc_sc, *, tq, lam_init):
    i = qi_ref[pl.program_id(1)]
    j = kj_ref[pl.program_id(1)]
    ones_rows = 16

    @pl.when(j == 0)
    def _():
        m_sc[...] = jnp.full(m_sc.shape, NEG_INF, F32)
        l_sc[...] = jnp.zeros(l_sc.shape, F32)
        acc_sc[...] = jnp.zeros(acc_sc.shape, F32)

    def step(masked):
        for h in range(N_DIFF_HEADS):
            hs = slice(h * V_HEAD_DIM, (h + 1) * V_HEAD_DIM)
            kh = k_ref[:, hs]
            for c, q_ref in enumerate((q1_ref, q2_ref)):
                s_sc[2 * h + c] = _dot_nt(kh, q_ref[:, hs])
        s = s_sc[...]
        if masked:
            key = lax.broadcasted_iota(jnp.int32, (tq, tq), 0)
            qry = lax.broadcasted_iota(jnp.int32, (tq, tq), 1)
            s = jnp.where((key <= qry)[None], s, NEG_INF)
        m_old = m_sc[...]
        m_new = jnp.maximum(m_old, jnp.max(s, axis=1, keepdims=True))
        alpha = jnp.exp2(m_old - m_new)
        p = jnp.exp2(s - m_new).astype(BF16)
        m_sc[...] = m_new
        ones = jnp.ones((ones_rows, tq), BF16)
        for h in range(N_DIFF_HEADS):
            lhs = jnp.concatenate([vt_ref[h * V_HEAD_DIM:(h + 1) * V_HEAD_DIM, :], ones], axis=0)
            for c in range(2):
                idx = 2 * h + c
                r = _dot(lhs, p[idx])
                acc_sc[idx] = alpha[idx] * acc_sc[idx] + r[0:V_HEAD_DIM]
                l_sc[idx] = alpha[idx] * l_sc[idx] + r[V_HEAD_DIM:V_HEAD_DIM + 1]

    @pl.when(j < i)
    def _():
        step(False)

    @pl.when(j == i)
    def _():
        step(True)
        lam = _lam(lq1_ref[...], lk1_ref[...], lq2_ref[...], lk2_ref[...], lam_init)
        sg = sg_ref[...]
        for h in range(N_DIFF_HEADS):
            o = acc_sc[2 * h] / l_sc[2 * h] - lam * (acc_sc[2 * h + 1] / l_sc[2 * h + 1])
            ms = jnp.mean(o * o, axis=0, keepdims=True)
            o = o * lax.rsqrt(ms + EPS) * sg * (1.0 - lam_init)
            o_ref[:, h * V_HEAD_DIM:(h + 1) * V_HEAD_DIM] = o.T.astype(BF16)


def _attn(q1, q2, kb, vt, lams, sg_col, layer, lam_init):
    b, t, w = q1.shape
    tq = TQ
    nq = t // tq
    n_combo = 2 * N_DIFF_HEADS
    pairs = [(i, j) for i in range(nq) for j in range(i + 1)]
    qi = jnp.asarray([p[0] for p in pairs], jnp.int32)
    kj = jnp.asarray([p[1] for p in pairs], jnp.int32)
    qspec = pl.BlockSpec((None, tq, w), lambda bi, s, qi, kj: (bi, qi[s], 0))
    kspec = pl.BlockSpec((None, tq, w), lambda bi, s, qi, kj: (bi, kj[s], 0))
    vspec = pl.BlockSpec((None, w, tq), lambda bi, s, qi, kj: (bi, 0, kj[s]))
    lspec = pl.BlockSpec((None, 1, DIFF_HEAD_DIM), lambda bi, s, qi, kj: (layer, 0, 0))
    return pl.pallas_call(
        functools.partial(_attn_kernel, tq=tq, lam_init=lam_init),
        grid_spec=pltpu.PrefetchScalarGridSpec(
            num_scalar_prefetch=2,
            grid=(b, len(pairs)),
            in_specs=[qspec, qspec, kspec, vspec, lspec, lspec, lspec, lspec,
                      pl.BlockSpec((None, V_HEAD_DIM, 1), lambda bi, s, qi, kj: (layer, 0, 0))],
            out_specs=pl.BlockSpec((None, tq, w), lambda bi, s, qi, kj: (bi, qi[s], 0)),
            scratch_shapes=[pltpu.VMEM((n_combo, tq, tq), F32),
                            pltpu.VMEM((n_combo, 1, tq), F32),
                            pltpu.VMEM((n_combo, 1, tq), F32),
                            pltpu.VMEM((n_combo, V_HEAD_DIM, tq), F32)]),
        out_shape=jax.ShapeDtypeStruct((b, t, w), BF16),
        compiler_params=_params(("arbitrary", "arbitrary")),
        name="diff_attn",
    )(qi, kj, q1, q2, kb, vt, *lams, sg_col)


def _cross_attend(qc, mk_ref, mv_ref):
    outs = []
    for h in range(N_MEM_HEADS):
        hs = slice(h * MEM_HEAD_DIM, (h + 1) * MEM_HEAD_DIM)
        s = _dot_nt(qc[:, hs], mk_ref[:, hs])
        m = jnp.max(s, axis=1, keepdims=True)
        p = jnp.exp(s - m)
        l = jnp.sum(p, axis=1, keepdims=True)
        outs.append((_dot(p.astype(BF16), mv_ref[:, hs]) / l).astype(BF16))
    return jnp.concatenate(outs, axis=1)


def _mix_out_kernel(x_ref, at_ref, co_ref, wo_ref, g_ref, wq_ref, mk_ref, mv_ref, wco_ref, o_ref,
                    wob, wqb, wcob):
    _cast_weights((pl.program_id(0) == 0) & (pl.program_id(1) == 0),
                  ((wo_ref, wob), (wq_ref, wqb), (wco_ref, wcob)))
    x1 = (x_ref[...] + _dot(at_ref[...], wob[0:ATTN_WIDTH, :])
          + _dot(co_ref[...], wob[ATTN_WIDTH:ATTN_WIDTH + CONV_WIDTH, :]))
    hq = _rms(x1, g_ref[...]).astype(BF16)
    qc = (_dot(hq, wqb[...]) * (MEM_HEAD_DIM ** -0.5)).astype(BF16)
    o = _cross_attend(qc, mk_ref, mv_ref)
    o_ref[...] = x1 + _dot(o, wcob[...])


def _mix_out(x, at, co, w_out, norm3, w_cq, mkb, mvb, w_co, layer):
    b, t, d = x.shape
    tm = TM_MIX
    n_mem = mkb.shape[2]
    wspec = pl.BlockSpec((None, d, d), lambda i, j: (layer, 0, 0), pipeline_mode=pl.Buffered(1))
    hspec = pl.BlockSpec((None, tm, ATTN_WIDTH), lambda i, j: (i, j, 0))
    mspec = pl.BlockSpec((None, None, n_mem, d), lambda i, j: (layer, i, 0, 0))
    xspec = pl.BlockSpec((None, tm, d), lambda i, j: (i, j, 0))
    return pl.pallas_call(
        _mix_out_kernel,
        grid=(b, t // tm),
        in_specs=[xspec, hspec, hspec, wspec,
                  pl.BlockSpec((None, 1, d), lambda i, j: (layer, 0, 0)),
                  wspec, mspec, mspec, wspec],
        out_specs=xspec,
        out_shape=jax.ShapeDtypeStruct((b, t, d), F32),
        scratch_shapes=[pltpu.VMEM((d, d), BF16)] * 3,
        compiler_params=_params(("arbitrary", "arbitrary")),
        name="mix_out",
    )(x, at, co, w_out, norm3, w_cq, mkb, mvb, w_co)


def _ffn_kernel(x_ref, g_ref, wa_ref, wg_ref, cwa_ref, cwg_ref, wd_ref, gf_ref,
                o_ref, hist_ref, hb_sc, acc_sc, buf, car_a, car_g, *, tm, final):
    t = pl.program_id(1)
    x = x_ref[...]
    hb_sc[...] = _rms(x, g_ref[...]).astype(BF16)
    acc_sc[...] = x

    @pl.when(t == 0)
    def _():
        car_a[...] = jnp.zeros(car_a.shape, F32)
        car_g[...] = jnp.zeros(car_g.shape, F32)

    def conv(u, stage, car, cw, c):
        stage[0:SUBLANES, :] = car[c]
        stage[SUBLANES:SUBLANES + tm, :] = u
        y = (cw[0:1] * stage[SUBLANES - 2:SUBLANES - 2 + tm, :]
             + cw[1:2] * stage[SUBLANES - 1:SUBLANES - 1 + tm, :]
             + cw[2:3] * u)
        car[c] = u[tm - SUBLANES:, :]
        return y

    def chunk(c, pair_slot):
        hb = hb_sc[...]
        a = conv(_dot(hb, wa_ref[c]), buf.at[2 * pair_slot], car_a, cwa_ref[c], c)
        g = conv(_dot(hb, wg_ref[c]), buf.at[2 * pair_slot + 1], car_g, cwg_ref[c], c)
        act = (jax.nn.silu(g) * a).astype(BF16)
        return _dot(act, wd_ref[c])

    def pair(k, carry):
        acc_sc[...] += chunk(2 * k, 0) + chunk(2 * k + 1, 1)
        return carry

    lax.fori_loop(0, N_FF_CHUNKS // 2, pair, 0)
    for c in range(N_FF_CHUNKS - N_FF_CHUNKS % 2, N_FF_CHUNKS):
        acc_sc[...] += chunk(c, 0)
    hist_ref[:, 0] = car_a[...]
    hist_ref[:, 1] = car_g[...]
    out = acc_sc[...]
    if final:
        out = _rms(out, gf_ref[...])
    o_ref[...] = out


def _ffn(x, norm3, wa, wg, cwa, cwg, wd, gf2, layer, final):
    b, t, d = x.shape
    tm = TM_FFN
    nch = N_FF_CHUNKS
    xspec = pl.BlockSpec((None, tm, d), lambda i, j: (i, j, 0))
    single = pl.Buffered(1)
    return pl.pallas_call(
        functools.partial(_ffn_kernel, tm=tm, final=final),
        grid=(b, t // tm),
        in_specs=[xspec,
                  pl.BlockSpec((None, 1, d), lambda i, j: (layer, 0, 0)),
                  pl.BlockSpec((None, nch, d, FF_CHUNK), lambda i, j: (layer, 0, 0, 0), pipeline_mode=single),
                  pl.BlockSpec((None, nch, d, FF_CHUNK), lambda i, j: (layer, 0, 0, 0), pipeline_mode=single),
                  pl.BlockSpec((None, nch, 3, FF_CHUNK), lambda i, j: (layer, 0, 0, 0)),
                  pl.BlockSpec((None, nch, 3, FF_CHUNK), lambda i, j: (layer, 0, 0, 0)),
                  pl.BlockSpec((None, nch, FF_CHUNK, d), lambda i, j: (layer, 0, 0, 0), pipeline_mode=single),
                  pl.BlockSpec((1, d), lambda i, j: (0, 0))],
        out_specs=[xspec,
                   pl.BlockSpec((None, nch, 2, SUBLANES, FF_CHUNK), lambda i, j: (i, 0, 0, 0, 0))],
        out_shape=[jax.ShapeDtypeStruct((b, t, d), F32),
                   jax.ShapeDtypeStruct((b, nch, 2, SUBLANES, FF_CHUNK), F32)],
        scratch_shapes=[pltpu.VMEM((tm, d), BF16),
                        pltpu.VMEM((tm, d), F32),
                        pltpu.VMEM((4, tm + SUBLANES, FF_CHUNK), F32),
                        pltpu.VMEM((nch, SUBLANES, FF_CHUNK), F32),
                        pltpu.VMEM((nch, SUBLANES, FF_CHUNK), F32)],
        compiler_params=_params(("arbitrary", "arbitrary")),
        name="ffn",
    )(x, norm3, wa, wg, cwa, cwg, wd, gf2)


def _dec_in_kernel(x_ref, g_ref, w_ref, cw_ref, hist_ref, cos_ref, sa_ref, sb_ref,
                   q_ref, k_ref, v_ref, co_ref, nh_ref):
    h = _rms(x_ref[...], g_ref[...]).astype(BF16)
    cos, sa, sb = cos_ref[...], sa_ref[...], sb_ref[...]

    def proj(i):
        return _dot(h, w_ref[:, i * ATTN_WIDTH:(i + 1) * ATTN_WIDTH].astype(BF16))

    q_ref[...] = _rope(proj(0), cos, sa, sb) * (DIFF_HEAD_DIM ** -0.5)
    k_ref[...] = _rope(proj(1), cos, sa, sb)
    v_ref[...] = proj(2)
    gate_b = proj(3)
    cu = proj(4) * proj(5)
    h0 = hist_ref[:, 0:CONV_WIDTH]
    h1 = hist_ref[:, CONV_WIDTH:2 * CONV_WIDTH]
    cw = cw_ref[...]
    y = cw[0:1] * h0 + cw[1:2] * h1 + cw[2:3] * cu
    co_ref[...] = (gate_b * y).astype(BF16)
    nh_ref[:, 0:CONV_WIDTH] = h1
    nh_ref[:, CONV_WIDTH:2 * CONV_WIDTH] = cu


def _dec_in(x, norm3, w_in, conv_w, hist, tables, layer):
    n, d = x.shape
    cos, sa, sb = tables
    full = lambda shape: pl.BlockSpec(shape, lambda i: (0,) * len(shape))
    ospec = full((n, ATTN_WIDTH))
    return pl.pallas_call(
        _dec_in_kernel,
        grid=(1,),
        in_specs=[full((n, d)),
                  pl.BlockSpec((None, 1, d), lambda i: (layer, 0, 0)),
                  pl.BlockSpec((None, d, w_in.shape[2]), lambda i: (layer, 0, 0)),
                  pl.BlockSpec((None, 3, CONV_WIDTH), lambda i: (layer, 0, 0)),
                  pl.BlockSpec((None, n, 2 * CONV_WIDTH), lambda i: (layer, 0, 0)),
                  full((1, LANES)), full((1, LANES)), full((1, LANES))],
        out_specs=[ospec, ospec, ospec, ospec, full((n, 2 * CONV_WIDTH))],
        out_shape=[jax.ShapeDtypeStruct((n, ATTN_WIDTH), F32)] * 3
        + [jax.ShapeDtypeStruct((n, CONV_WIDTH), BF16),
           jax.ShapeDtypeStruct((n, 2 * CONV_WIDTH), F32)],
        compiler_params=_params(("arbitrary",)),
        name="dec_in",
    )(x, norm3, w_in, conv_w, hist, cos, sa, sb)


def _group_select(n_rows, width, group):
    r = lax.broadcasted_iota(jnp.int32, (n_rows, width), 0)
    c = lax.broadcasted_iota(jnp.int32, (n_rows, width), 1)
    return r == c // group


def _dec_attn_kernel(pt_ref, q_ref, kn_ref, vn_ref, lq1_ref, lk1_ref, lq2_ref, lk2_ref, sg_ref,
                     kt_hbm, v_hbm, o_ref, buf, sem, s_sc, p_sc, x_sc,
                     *, layer, lam_init, n_req, n_pages, page):
    cp = PAGES_PER_CHUNK
    nch = n_pages // cp
    width = ATTN_WIDTH
    n_combo = 2 * N_DIFF_HEADS
    jobs = 2 * nch

    def copies(b, job, slot):
        src = kt_hbm if job < nch else v_hbm
        c = job % nch
        return [pltpu.make_async_copy(src.at[layer, pt_ref[b, c * cp + g]], buf.at[slot, g], sem.at[slot])
                for g in range(cp)]

    def start(b, job, slot):
        for cpy in copies(b, job, slot):
            cpy.start()

    def wait(b, job, slot):
        for cpy in copies(b, job, slot):
            cpy.wait()

    lam = _lam(lq1_ref[...], lk1_ref[...], lq2_ref[...], lk2_ref[...], lam_init)
    sg = sg_ref[...]
    combo = lax.broadcasted_iota(jnp.int32, (n_combo, 1), 0)
    feat_sel = _group_select(n_combo, width, DIFF_HEAD_DIM)
    r = lax.broadcasted_iota(jnp.int32, (page, N_DIFF_HEADS * page), 0)
    c = lax.broadcasted_iota(jnp.int32, (page, N_DIFF_HEADS * page), 1)
    dilate = jnp.where(c // N_DIFF_HEADS == r, 1.0, 0.0).astype(BF16)
    rk = lax.broadcasted_iota(jnp.int32, (cp * n_combo, N_DIFF_HEADS * page), 0) % n_combo
    ch = lax.broadcasted_iota(jnp.int32, (cp * n_combo, N_DIFF_HEADS * page), 1) % N_DIFF_HEADS
    head_sel = rk // 2 == ch

    ahead = DEC_ATTN_SLOTS - 1

    def slot_of(b, job):
        return lax.rem(b * jobs + job, DEC_ATTN_SLOTS)

    for first in range(ahead):
        start(0, first, first)

    def request(b, carry):
        qrow = q_ref[b]
        qcol = jnp.broadcast_to(qrow, (page, width)).T
        vn = vn_ref[b]
        s_self = jnp.sum(jnp.where(feat_sel, jnp.broadcast_to(qrow * kn_ref[b], (n_combo, width)), 0.0),
                         axis=1, keepdims=True)
        o8 = None
        for job in range(jobs):
            slot = slot_of(b, job)
            nxt = job + ahead
            if nxt < jobs:
                start(b, nxt, slot_of(b, nxt))
            else:
                @pl.when(b + 1 < n_req)
                def _(nxt=nxt):
                    start(b + 1, nxt - jobs, slot_of(b + 1, nxt - jobs))
            wait(b, job, slot)
            if job < nch:
                def k_page(g, carry2, job=job, slot=slot):
                    prod = buf[slot, g] * qcol
                    s_sc[job * cp + g] = jnp.sum(prod.reshape(n_combo, DIFF_HEAD_DIM, page), axis=1)
                    return carry2

                lax.fori_loop(0, cp, k_page, 0)
                if job == nch - 1:
                    s = s_sc[...]
                    m = jnp.maximum(jnp.max(jnp.max(s, axis=0), axis=1, keepdims=True), s_self)
                    p = jnp.exp(s - m)
                    p_self = jnp.exp(s_self - m)
                    l = jnp.sum(jnp.sum(p, axis=0), axis=1, keepdims=True) + p_self
                    scale = jnp.where(combo % 2 == 0, 1.0, -lam) / l
                    p_sc[...] = p * scale
                    vn8 = jnp.concatenate(
                        [vn[:, (k // 2) * V_HEAD_DIM:(k // 2 + 1) * V_HEAD_DIM] for k in range(n_combo)], axis=0)
                    o8 = (p_self * scale) * vn8
            else:
                jv = job - nch
                pch = p_sc[jv * cp:(jv + 1) * cp].reshape(cp * n_combo, page).astype(BF16)
                x_sc[...] = jnp.where(head_sel, _dot(pch, dilate), 0.0)

                def v_page(g, acc, slot=slot):
                    xg = x_sc[pl.ds(pl.multiple_of(g * n_combo, n_combo), n_combo), :].astype(BF16)
                    return acc + _dot(xg, buf[slot, g].astype(BF16))

                o8 = lax.fori_loop(0, cp, v_page, o8, unroll=4)

        o_ref[b] = jnp.concatenate(
            [_rms(o8[2 * h:2 * h + 1] + o8[2 * h + 1:2 * h + 2], sg) * (1.0 - lam_init)
             for h in range(N_DIFF_HEADS)], axis=1)
        return carry

    lax.fori_loop(0, n_req, request, 0)


def _dec_attn(page_table, q, kn, vn, lams, sg3, cache_kt, cache_v2, layer, lam_init):
    n_req, n_pages = page_table.shape
    page = cache_kt.shape[3]
    cp = PAGES_PER_CHUNK
    n_combo = 2 * N_DIFF_HEADS
    vmem = pl.BlockSpec(memory_space=pltpu.VMEM)
    smem = pl.BlockSpec(memory_space=pltpu.SMEM)
    hbm = pl.BlockSpec(memory_space=pl.ANY)
    return pl.pallas_call(
        functools.partial(_dec_attn_kernel, layer=layer, lam_init=lam_init, n_req=n_req,
                          n_pages=n_pages, page=page),
        in_specs=[smem, vmem, vmem, vmem, vmem, vmem, vmem, vmem, vmem, hbm, hbm],
        out_specs=vmem,
        out_shape=jax.ShapeDtypeStruct((n_req, 1, ATTN_WIDTH), F32),
        scratch_shapes=[pltpu.VMEM((DEC_ATTN_SLOTS, cp, ATTN_WIDTH, page), F32),
                        pltpu.SemaphoreType.DMA((DEC_ATTN_SLOTS,)),
                        pltpu.VMEM((n_pages, n_combo, page), F32),
                        pltpu.VMEM((n_pages, n_combo, page), F32),
                        pltpu.VMEM((cp * n_combo, N_DIFF_HEADS * page), F32)],
        compiler_params=pltpu.CompilerParams(vmem_limit_bytes=VMEM_LIMIT_BYTES),
        name="dec_attn",
    )(page_table, *[a.reshape(n_req, 1, ATTN_WIDTH) for a in (q, kn, vn)],
      *[a[layer] for a in lams], sg3[layer], cache_kt, cache_v2).reshape(n_req, ATTN_WIDTH)


def _dec_mid_kernel(x_ref, at_ref, co_ref, wo_ref, g_ref, wq_ref, x1_ref, qc_ref):
    x1 = (x_ref[...] + _dot(at_ref[...].astype(BF16), wo_ref[0:ATTN_WIDTH, :].astype(BF16))
          + _dot(co_ref[...], wo_ref[ATTN_WIDTH:ATTN_WIDTH + CONV_WIDTH, :].astype(BF16)))
    x1_ref[...] = x1
    hq = _rms(x1, g_ref[...]).astype(BF16)
    qc_ref[...] = _dot(hq, wq_ref[...].astype(BF16)) * (MEM_HEAD_DIM ** -0.5)


def _dec_mid(x, at, co, w_out, norm3, w_cq, layer):
    n, d = x.shape
    full = lambda shape: pl.BlockSpec(shape, lambda i: (0,) * len(shape))
    wspec = pl.BlockSpec((None, d, d), lambda i: (layer, 0, 0))
    return pl.pallas_call(
        _dec_mid_kernel,
        grid=(1,),
        in_specs=[full((n, d)), full((n, ATTN_WIDTH)), full((n, CONV_WIDTH)), wspec,
                  pl.BlockSpec((None, 1, d), lambda i: (layer, 0, 0)), wspec],
        out_specs=[full((n, d)), full((n, d))],
        out_shape=[jax.ShapeDtypeStruct((n, d), F32)] * 2,
        compiler_params=_params(("arbitrary",)),
        name="dec_mid",
    )(x, at, co, w_out, norm3, w_cq)


def _dec_cross_kernel(q_ref, mk_ref, mv_ref, o_ref):
    tile = (mk_ref.shape[0] // SUBLANES, SUBLANES, LANES)
    r = jnp.sum(mk_ref[...].reshape(tile) * q_ref[...][None], axis=2, keepdims=True)
    s = r + pltpu.roll(r, N_MEM_HEADS, 1)
    m = jnp.max(s, axis=0, keepdims=True)
    p = jnp.exp(s - m)
    l = jnp.sum(p, axis=0, keepdims=True)
    o_ref[...] = jnp.sum((p / l) * mv_ref[...].reshape(tile), axis=0)


def _dec_cross(q8, mem_k, mem_v, layer):
    n = q8.shape[0]
    rows = mem_k.shape[2]
    qspec = pl.BlockSpec((None, SUBLANES, LANES), lambda i: (i, 0, 0))
    mspec = pl.BlockSpec((None, None, rows, LANES), lambda i: (layer, i, 0, 0))
    return pl.pallas_call(
        _dec_cross_kernel,
        grid=(n,),
        in_specs=[qspec, mspec, mspec],
        out_specs=qspec,
        out_shape=jax.ShapeDtypeStruct((n, SUBLANES, LANES), F32),
        compiler_params=_params(("arbitrary",)),
        name="dec_cross",
    )(q8, mem_k, mem_v)


def _dec_ffn_kernel(x1_ref, oc_ref, wco_ref, g_ref, wa_ref, wg_ref, cwa_ref, cwg_ref, wd_ref,
                    hist_ref, gf_ref, o_ref, nh_ref, *, final):
    x2 = x1_ref[...] + _dot(oc_ref[...].astype(BF16), wco_ref[...].astype(BF16))
    hb = _rms(x2, g_ref[...]).astype(BF16)
    acc = x2
    for c in range(N_FF_CHUNKS):
        ys = []
        for half, (w_ref, cw_ref) in enumerate(((wa_ref, cwa_ref), (wg_ref, cwg_ref))):
            lo = half * D_FF + c * FF_CHUNK
            u = _dot(hb, w_ref[c])
            h0 = hist_ref[:, lo:lo + FF_CHUNK]
            h1 = hist_ref[:, 2 * D_FF + lo:2 * D_FF + lo + FF_CHUNK]
            cw = cw_ref[c]
            ys.append(cw[0:1] * h0 + cw[1:2] * h1 + cw[2:3] * u)
            nh_ref[:, lo:lo + FF_CHUNK] = h1
            nh_ref[:, 2 * D_FF + lo:2 * D_FF + lo + FF_CHUNK] = u
        act = (jax.nn.silu(ys[1]) * ys[0]).astype(BF16)
        acc = acc + _dot(act, wd_ref[c])
    if final:
        acc = _rms(acc, gf_ref[...])
    o_ref[...] = acc


def _dec_ffn(x1, oc, w_co, norm3, wa, wg, cwa, cwg, wd, hist, gf2, layer, final):
    n, d = x1.shape
    nch = N_FF_CHUNKS
    full = lambda shape: pl.BlockSpec(shape, lambda i: (0,) * len(shape))
    return pl.pallas_call(
        functools.partial(_dec_ffn_kernel, final=final),
        grid=(1,),
        in_specs=[full((n, d)), full((n, d)),
                  pl.BlockSpec((None, d, d), lambda i: (layer, 0, 0)),
                  pl.BlockSpec((None, 1, d), lambda i: (layer, 0, 0)),
                  pl.BlockSpec((None, nch, d, FF_CHUNK), lambda i: (layer, 0, 0, 0)),
                  pl.BlockSpec((None, nch, d, FF_CHUNK), lambda i: (layer, 0, 0, 0)),
                  pl.BlockSpec((None, nch, 3, FF_CHUNK), lambda i: (layer, 0, 0, 0)),
                  pl.BlockSpec((None, nch, 3, FF_CHUNK), lambda i: (layer, 0, 0, 0)),
                  pl.BlockSpec((None, nch, FF_CHUNK, d), lambda i: (layer, 0, 0, 0)),
                  pl.BlockSpec((None, n, 4 * D_FF), lambda i: (layer, 0, 0)),
                  full((1, d))],
        out_specs=[full((n, d)), full((n, 4 * D_FF))],
        out_shape=[jax.ShapeDtypeStruct((n, d), F32), jax.ShapeDtypeStruct((n, 4 * D_FF), F32)],
        compiler_params=_params(("arbitrary",)),
        name="dec_ffn",
    )(x1, oc, w_co, norm3, wa, wg, cwa, cwg, wd, hist, gf2)


def kernel(x_prompt, x_sample, mem_prompt, cache_k, cache_v, state_conv, state_ffn_conv,
           cache_mem_k, cache_mem_v, page_table, w_in, w_out, conv_w, lambda_q1, lambda_k1,
           lambda_q2, lambda_k2, subln_g, norm_mix, norm_cross, norm_mem, w_cq, w_ck, w_cv,
           w_co, norm_ffn, w_up, ffn_conv_w, w_down, norm_final):
    depth = w_in.shape[0]
    bp, tp, d = x_prompt.shape
    bs, ts, _ = x_sample.shape
    n_pages = page_table.shape[1]
    page = cache_k.shape[2]
    past_len = n_pages * page
    n_phys = cache_k.shape[1]
    nch = N_FF_CHUNKS

    bf = lambda w: w.astype(BF16)
    chunked = lambda w: bf(w).reshape(depth, d, nch, FF_CHUNK).transpose(0, 2, 1, 3)
    wa_b = chunked(w_up[:, :, :D_FF])
    wg_b = chunked(w_up[:, :, D_FF:])
    wd_b = bf(w_down).reshape(depth, nch, FF_CHUNK, d)
    cw_chunked = lambda w: w.reshape(depth, 3, nch, FF_CHUNK).transpose(0, 2, 1, 3)
    cwa = cw_chunked(ffn_conv_w[:, :, :D_FF])
    cwg = cw_chunked(ffn_conv_w[:, :, D_FF:])
    row3 = lambda g: g.reshape(depth, 1, g.shape[-1])
    norm_mix3, norm_cross3, norm_mem3, norm_ffn3, subln3 = map(
        row3, (norm_mix, norm_cross, norm_mem, norm_ffn, subln_g))
    lams = tuple(map(row3, (lambda_q1, lambda_k1, lambda_q2, lambda_k2)))
    gf2 = norm_final.reshape(1, d)
    subln_col = subln_g.reshape(depth, V_HEAD_DIM, 1)

    tables_p = _rope_tables(jnp.arange(tp))
    tables_s = _rope_tables(past_len + jnp.arange(ts))

    cache_k4 = cache_k.transpose(0, 1, 3, 4, 5, 2).reshape(depth, n_phys, ATTN_WIDTH, page)
    cache_v4 = cache_v.reshape(depth, n_phys, N_DIFF_HEADS * page, V_HEAD_DIM)
    halves = MEM_HEAD_DIM // LANES
    mem_rows = lambda a: a.reshape(depth, bs, a.shape[2], N_MEM_HEADS, halves, LANES).transpose(
        0, 1, 2, 4, 3, 5).reshape(depth, bs, a.shape[2] * halves * N_MEM_HEADS, LANES)
    mem_k4 = mem_rows(cache_mem_k)
    mem_v4 = mem_rows(cache_mem_v)
    to_tile = lambda a: a.reshape(bs, N_MEM_HEADS, halves, LANES).transpose(0, 2, 1, 3).reshape(
        bs, halves * N_MEM_HEADS, LANES)
    from_tile = lambda a: a.reshape(bs, halves, N_MEM_HEADS, LANES).transpose(0, 2, 1, 3).reshape(bs, d)
    conv_hist = state_conv.reshape(depth, bs, 2 * CONV_WIDTH)
    ffn_hist = state_ffn_conv.reshape(depth, bs, 4 * D_FF)

    mk, mv, mkb, mvb = _memkv(mem_prompt, norm_mem3, w_ck, w_cv)

    xp = x_prompt
    xs = x_sample.reshape(bs * ts, d)
    kp_l, vp_l, cp_l, fp_l, ks_l, vs_l, cs_l, fs_l = ([] for _ in range(8))
    for l in range(depth):
        lam_init = 0.8 - 0.6 * math.exp(-0.3 * l)
        final = l == depth - 1
        q1, q2, kt, v4, kb, vt, co, chist = _mix_in(xp, norm_mix3, w_in, conv_w, tables_p, l)
        at = _attn(q1, q2, kb, vt, lams, subln_col, l, lam_init)
        x2 = _mix_out(xp, at, co, w_out, norm_cross3, w_cq, mkb, mvb, w_co, l)
        xp, fhist = _ffn(x2, norm_ffn3, wa_b, wg_b, cwa, cwg, wd_b, gf2, l, final)
        kp_l.append(kt)
        vp_l.append(v4)
        cp_l.append(chist[:, SUBLANES - 2:, :])
        fp_l.append(fhist[:, :, :, SUBLANES - 2:, :].transpose(0, 3, 2, 1, 4).reshape(bp, 2, 2 * D_FF))
        qd, kd, vd, cod, nconv = _dec_in(xs, norm_mix3, w_in, conv_w, conv_hist, tables_s, l)
        atd = _dec_attn(page_table, qd, kd, vd, lams, subln3, cache_k4, cache_v4, l, lam_init)
        x1d, qc = _dec_mid(xs, atd, cod, w_out, norm_cross3, w_cq, l)
        oc = from_tile(_dec_cross(to_tile(qc), mem_k4, mem_v4, l))
        xs, nffn = _dec_ffn(x1d, oc, w_co, norm_ffn3, wa_b, wg_b, cwa, cwg, wd_b, ffn_hist, gf2, l, final)
        ks_l.append(kd)
        vs_l.append(vd)
        cs_l.append(nconv)
        fs_l.append(nffn)

    st = jnp.stack
    return (xp,
            xs.reshape(bs, ts, d),
            st(kp_l).reshape(depth, bp, N_DIFF_HEADS, 2, DIFF_HEAD_DIM, tp).transpose(0, 1, 5, 2, 3, 4),
            st(vp_l).reshape(depth, bp, tp, N_DIFF_HEADS, V_HEAD_DIM),
            st(cp_l),
            st(fp_l),
            mk.reshape(depth, bp, -1, N_MEM_HEADS, MEM_HEAD_DIM),
            mv.reshape(depth, bp, -1, N_MEM_HEADS, MEM_HEAD_DIM),
            st(ks_l).reshape(depth, bs, ts, N_DIFF_HEADS, 2, DIFF_HEAD_DIM),
            st(vs_l).reshape(depth, bs, ts, N_DIFF_HEADS, V_HEAD_DIM),
            st(cs_l).reshape(depth, bs, 2, CONV_WIDTH),
            st(fs_l).reshape(depth, bs, 2, 2 * D_FF))
```

```python
import functools
import math

import jax
import jax.numpy as jnp
from jax import lax
from jax.experimental import pallas as pl
from jax.experimental.pallas import tpu as pltpu

F32 = jnp.float32
BF16 = jnp.bfloat16

D_MODEL = 1024
ATTN_WIDTH = 512
CONV_WIDTH = 512
N_DIFF_HEADS = 4
DIFF_HEAD_DIM = 64
V_HEAD_DIM = 2 * DIFF_HEAD_DIM
ROT_DIM = DIFF_HEAD_DIM // 4
ROPE_THETA = 500000.0
N_MEM_HEADS = 4
MEM_HEAD_DIM = D_MODEL // N_MEM_HEADS
D_FF = 11 * D_MODEL // 4
EPS = 1e-6
NEG_INF = -1e30
LOG2E = math.log2(math.e)

LANES = 128
SUBLANES = 8
VMEM_LIMIT_BYTES = 56 * 1024 * 1024

TM_MIX = 1024
TQ = 512
TM_FFN = 1024
FF_CHUNK = 256
N_FF_CHUNKS = D_FF // FF_CHUNK
FF_GROUP = 4
PAGES_PER_CHUNK = 16
DEC_ATTN_SLOTS = 3


def _dot(a, b):
    return jnp.dot(a, b, preferred_element_type=F32)


def _dot_nt(a, b):
    return lax.dot_general(a, b, (((1,), (1,)), ((), ())), preferred_element_type=F32)


def _rms(x, g):
    ms = jnp.mean(x * x, axis=-1, keepdims=True)
    return x * lax.rsqrt(ms + EPS) * g


def _rope(y, cos, sa, sb):
    outs = []
    for c in range(y.shape[1] // LANES):
        yc = y[:, c * LANES:(c + 1) * LANES]
        outs.append(yc * cos + pltpu.roll(yc, ROT_DIM // 2, 1) * sa
                    + pltpu.roll(yc, LANES - ROT_DIM // 2, 1) * sb)
    return jnp.concatenate(outs, axis=1)


def _rope_tables(pos):
    half = ROT_DIM // 2
    inv = jnp.power(ROPE_THETA, -jnp.arange(half, dtype=F32) * 2.0 / ROT_DIM)
    ang = pos.astype(F32)[:, None] * inv[None, :]
    cos, sin = jnp.cos(ang), jnp.sin(ang)
    n = pos.shape[0]
    ones = jnp.ones((n, DIFF_HEAD_DIM - ROT_DIM), F32)
    zeros = jnp.zeros_like(ones)
    z8 = jnp.zeros((n, half), F32)
    c64 = jnp.concatenate([cos, cos, ones], axis=1)
    sa64 = jnp.concatenate([z8, sin, zeros], axis=1)
    sb64 = jnp.concatenate([-sin, z8, zeros], axis=1)
    tile2 = lambda a: jnp.concatenate([a, a], axis=1)
    return tile2(c64), tile2(sa64), tile2(sb64)


def _lam(lq1, lk1, lq2, lk2, lam_init):
    return (jnp.exp(jnp.sum(lq1 * lk1, axis=-1, keepdims=True))
            - jnp.exp(jnp.sum(lq2 * lk2, axis=-1, keepdims=True)) + lam_init)


def _params(sem=None):
    return pltpu.CompilerParams(dimension_semantics=sem, vmem_limit_bytes=VMEM_LIMIT_BYTES)


def _cast_weights(first, pairs):
    @pl.when(first)
    def _():
        for src, dst in pairs:
            dst[...] = src[...].astype(BF16)


def _memkv_kernel(mem_ref, g_ref, wk_ref, wv_ref, mk_ref, mv_ref, mkb_ref, mvb_ref, wkb, wvb):
    _cast_weights(pl.program_id(1) == 0, ((wk_ref, wkb), (wv_ref, wvb)))
    hm = _rms(mem_ref[...], g_ref[...]).astype(BF16)
    mk = _dot(hm, wkb[...])
    mv = _dot(hm, wvb[...])
    mk_ref[...] = mk
    mv_ref[...] = mv
    mkb_ref[...] = mk.astype(BF16)
    mvb_ref[...] = mv.astype(BF16)


def _memkv(mem, g3, wk, wv):
    depth = wk.shape[0]
    b, n, d = mem.shape
    wspec = pl.BlockSpec((None, d, d), lambda l, i: (l, 0, 0))
    ospec = pl.BlockSpec((None, None, n, d), lambda l, i: (l, i, 0, 0))
    return pl.pallas_call(
        _memkv_kernel,
        grid=(depth, b),
        in_specs=[pl.BlockSpec((None, n, d), lambda l, i: (i, 0, 0)),
                  pl.BlockSpec((None, 1, d), lambda l, i: (l, 0, 0)),
                  wspec, wspec],
        out_specs=[ospec, ospec, ospec, ospec],
        out_shape=[jax.ShapeDtypeStruct((depth, b, n, d), F32)] * 2
        + [jax.ShapeDtypeStruct((depth, b, n, d), BF16)] * 2,
        scratch_shapes=[pltpu.VMEM((d, d), BF16)] * 2,
        compiler_params=_params(("arbitrary", "arbitrary")),
        name="memkv",
    )(mem, g3, wk, wv)


def _mix_in_kernel(x_ref, g_ref, w_ref, cw_ref, cos_ref, sa_ref, sb_ref,
                   q1_ref, q2_ref, kt_ref, v_ref, kb_ref, vt_ref, co_ref, hist_ref,
                   cbuf, wb, *, tm):
    t = pl.program_id(1)
    _cast_weights((pl.program_id(0) == 0) & (t == 0), ((w_ref, wb),))
    h = _rms(x_ref[...], g_ref[...]).astype(BF16)
    cos, sa, sb = cos_ref[...], sa_ref[...], sb_ref[...]

    def proj(i):
        return _dot(h, wb[:, i * ATTN_WIDTH:(i + 1) * ATTN_WIDTH])

    lane = lax.broadcasted_iota(jnp.int32, (tm, ATTN_WIDTH), 1)
    first_map = (lane % V_HEAD_DIM) < DIFF_HEAD_DIM
    q = _rope(proj(0), cos, sa, sb) * (DIFF_HEAD_DIM ** -0.5 * LOG2E)
    q1_ref[...] = jnp.where(first_map, q, 0.0).astype(BF16)
    q2_ref[...] = jnp.where(first_map, 0.0, q).astype(BF16)
    k = _rope(proj(1), cos, sa, sb)
    kt_ref[...] = k.T
    kb_ref[...] = k.astype(BF16)
    v = proj(2)
    for hd in range(N_DIFF_HEADS):
        v_ref[pl.ds(hd, tm, stride=N_DIFF_HEADS), :] = v[:, hd * V_HEAD_DIM:(hd + 1) * V_HEAD_DIM]
    vt_ref[...] = v.T.astype(BF16)

    gate_b = proj(3)
    cu = proj(4) * proj(5)

    @pl.when(t == 0)
    def _():
        cbuf[0:SUBLANES, :] = jnp.zeros((SUBLANES, CONV_WIDTH), F32)

    cbuf[SUBLANES:SUBLANES + tm, :] = cu
    cw = cw_ref[...]
    y = (cw[0:1] * cbuf[SUBLANES - 2:SUBLANES - 2 + tm, :]
         + cw[1:2] * cbuf[SUBLANES - 1:SUBLANES - 1 + tm, :]
         + cw[2:3] * cu)
    co_ref[...] = (gate_b * y).astype(BF16)
    tail = cu[tm - SUBLANES:, :]
    hist_ref[...] = tail
    cbuf[0:SUBLANES, :] = tail


def _mix_in(x, norm3, w_in, conv_w, tables, layer):
    b, t, d = x.shape
    tm = TM_MIX
    cos, sa, sb = tables
    tok = lambda w, dt: jax.ShapeDtypeStruct((b, t, w), dt)
    tspec = pl.BlockSpec((tm, LANES), lambda i, j: (j, 0))
    ospec = pl.BlockSpec((None, tm, ATTN_WIDTH), lambda i, j: (i, j, 0))
    tspec_t = pl.BlockSpec((None, ATTN_WIDTH, tm), lambda i, j: (i, 0, j))
    return pl.pallas_call(
        functools.partial(_mix_in_kernel, tm=tm),
        grid=(b, t // tm),
        in_specs=[pl.BlockSpec((None, tm, d), lambda i, j: (i, j, 0)),
                  pl.BlockSpec((None, 1, d), lambda i, j: (layer, 0, 0)),
                  pl.BlockSpec((None, d, w_in.shape[2]), lambda i, j: (layer, 0, 0), pipeline_mode=pl.Buffered(1)),
                  pl.BlockSpec((None, 3, CONV_WIDTH), lambda i, j: (layer, 0, 0)),
                  tspec, tspec, tspec],
        out_specs=[ospec, ospec, tspec_t,
                   pl.BlockSpec((None, N_DIFF_HEADS * tm, V_HEAD_DIM), lambda i, j: (i, j, 0)),
                   ospec, tspec_t, ospec,
                   pl.BlockSpec((None, SUBLANES, CONV_WIDTH), lambda i, j: (i, 0, 0))],
        out_shape=[tok(ATTN_WIDTH, BF16), tok(ATTN_WIDTH, BF16),
                   jax.ShapeDtypeStruct((b, ATTN_WIDTH, t), F32),
                   jax.ShapeDtypeStruct((b, N_DIFF_HEADS * t, V_HEAD_DIM), F32),
                   tok(ATTN_WIDTH, BF16), jax.ShapeDtypeStruct((b, ATTN_WIDTH, t), BF16), tok(CONV_WIDTH, BF16),
                   jax.ShapeDtypeStruct((b, SUBLANES, CONV_WIDTH), F32)],
        scratch_shapes=[pltpu.VMEM((tm + SUBLANES, CONV_WIDTH), F32),
                        pltpu.VMEM((d, w_in.shape[2]), BF16)],
        compiler_params=_params(("arbitrary", "arbitrary")),
        name="mix_in",
    )(x, norm3, w_in, conv_w, cos, sa, sb)


def _attn_kernel(qi_ref, kj_ref, q1_ref, q2_ref, k_ref, vt_ref, lq1_ref, lk1_ref, lq2_ref, lk2_ref, sg_ref,
                 o_ref, s_sc, m_sc, l_sc, acc_sc, *, tq, lam_init):
    i = qi_ref[pl.program_id(1)]
    j = kj_ref[pl.program_id(1)]
    ones_rows = 16

    @pl.when(j == 0)
    def _():
        m_sc[...] = jnp.full(m_sc.shape, NEG_INF, F32)
        l_sc[...] = jnp.zeros(l_sc.shape, F32)
        acc_sc[...] = jnp.zeros(acc_sc.shape, F32)

    def step(masked):
        for h in range(N_DIFF_HEADS):
            hs = slice(h * V_HEAD_DIM, (h + 1) * V_HEAD_DIM)
            kh = k_ref[:, hs]
            for c, q_ref in enumerate((q1_ref, q2_ref)):
                s_sc[2 * h + c] = _dot_nt(kh, q_ref[:, hs])
        s = s_sc[...]
        if masked:
            key = lax.broadcasted_iota(jnp.int32, (tq, tq), 0)
            qry = lax.broadcasted_iota(jnp.int32, (tq, tq), 1)
            s = jnp.where((key <= qry)[None], s, NEG_INF)
        m_old = m_sc[...]
        m_new = jnp.maximum(m_old, jnp.max(s, axis=1, keepdims=True))
        alpha = jnp.exp2(m_old - m_new)
        p = jnp.exp2(s - m_new).astype(BF16)
        m_sc[...] = m_new
        ones = jnp.ones((ones_rows, tq), BF16)
        for h in range(N_DIFF_HEADS):
            lhs = jnp.concatenate([vt_ref[h * V_HEAD_DIM:(h + 1) * V_HEAD_DIM, :], ones], axis=0)
            for c in range(2):
                idx = 2 * h + c
                r = _dot(lhs, p[idx])
                acc_sc[idx] = alpha[idx] * acc_sc[idx] + r[0:V_HEAD_DIM]
                l_sc[idx] = alpha[idx] * l_sc[idx] + r[V_HEAD_DIM:V_HEAD_DIM + 1]

    @pl.when(j < i)
    def _():
        step(False)

    @pl.when(j == i)
    def _():
        step(True)
        lam = _lam(lq1_ref[...], lk1_ref[...], lq2_ref[...], lk2_ref[...], lam_init)
        sg = sg_ref[...]
        for h in range(N_DIFF_HEADS):
            o = acc_sc[2 * h] / l_sc[2 * h] - lam * (acc_sc[2 * h + 1] / l_sc[2 * h + 1])
            ms = jnp.mean(o * o, axis=0, keepdims=True)
            o = o * lax.rsqrt(ms + EPS) * sg * (1.0 - lam_init)
            o_ref[:, h * V_HEAD_DIM:(h + 1) * V_HEAD_DIM] = o.T.astype(BF16)


def _attn(q1, q2, kb, vt, lams, sg_col, layer, lam_init):
    b, t, w = q1.shape
    tq = TQ
    nq = t // tq
    n_combo = 2 * N_DIFF_HEADS
    pairs = [(i, j) for i in range(nq) for j in range(i + 1)]
    qi = jnp.asarray([p[0] for p in pairs], jnp.int32)
    kj = jnp.asarray([p[1] for p in pairs], jnp.int32)
    qspec = pl.BlockSpec((None, tq, w), lambda bi, s, qi, kj: (bi, qi[s], 0))
    kspec = pl.BlockSpec((None, tq, w), lambda bi, s, qi, kj: (bi, kj[s], 0))
    vspec = pl.BlockSpec((None, w, tq), lambda bi, s, qi, kj: (bi, 0, kj[s]))
    lspec = pl.BlockSpec((None, 1, DIFF_HEAD_DIM), lambda bi, s, qi, kj: (layer, 0, 0))
    return pl.pallas_call(
        functools.partial(_attn_kernel, tq=tq, lam_init=lam_init),
        grid_spec=pltpu.PrefetchScalarGridSpec(
            num_scalar_prefetch=2,
            grid=(b, len(pairs)),
            in_specs=[qspec, qspec, kspec, vspec, lspec, lspec, lspec, lspec,
                      pl.BlockSpec((None, V_HEAD_DIM, 1), lambda bi, s, qi, kj: (layer, 0, 0))],
            out_specs=pl.BlockSpec((None, tq, w), lambda bi, s, qi, kj: (bi, qi[s], 0)),
            scratch_shapes=[pltpu.VMEM((n_combo, tq, tq), F32),
                            pltpu.VMEM((n_combo, 1, tq), F32),
                            pltpu.VMEM((n_combo, 1, tq), F32),
                            pltpu.VMEM((n_combo, V_HEAD_DIM, tq), F32)]),
        out_shape=jax.ShapeDtypeStruct((b, t, w), BF16),
        compiler_params=_params(("arbitrary", "arbitrary")),
        name="diff_attn",
    )(qi, kj, q1, q2, kb, vt, *lams, sg_col)


def _cross_attend(qc, mk_ref, mv_ref):
    outs = []
    for h in range(N_MEM_HEADS):
        hs = slice(h * MEM_HEAD_DIM, (h + 1) * MEM_HEAD_DIM)
        s = _dot_nt(qc[:, hs], mk_ref[:, hs])
        m = jnp.max(s, axis=1, keepdims=True)
        p = jnp.exp(s - m)
        l = jnp.sum(p, axis=1, keepdims=True)
        outs.append((_dot(p.astype(BF16), mv_ref[:, hs]) / l).astype(BF16))
    return jnp.concatenate(outs, axis=1)


def _mix_out_kernel(x_ref, at_ref, co_ref, wo_ref, g_ref, wq_ref, mk_ref, mv_ref, wco_ref, o_ref,
                    wob, wqb, wcob):
    _cast_weights((pl.program_id(0) == 0) & (pl.program_id(1) == 0),
                  ((wo_ref, wob), (wq_ref, wqb), (wco_ref, wcob)))
    x1 = x_ref[...] + _dot(jnp.concatenate([at_ref[...], co_ref[...]], axis=1), wob[...])
    hq = _rms(x1, g_ref[...]).astype(BF16)
    qc = (_dot(hq, wqb[...]) * (MEM_HEAD_DIM ** -0.5)).astype(BF16)
    o = _cross_attend(qc, mk_ref, mv_ref)
    o_ref[...] = x1 + _dot(o, wcob[...])


def _mix_out(x, at, co, w_out, norm3, w_cq, mkb, mvb, w_co, layer):
    b, t, d = x.shape
    tm = TM_MIX
    n_mem = mkb.shape[2]
    wspec = pl.BlockSpec((None, d, d), lambda i, j: (layer, 0, 0), pipeline_mode=pl.Buffered(1))
    hspec = pl.BlockSpec((None, tm, ATTN_WIDTH), lambda i, j: (i, j, 0))
    mspec = pl.BlockSpec((None, None, n_mem, d), lambda i, j: (layer, i, 0, 0))
    xspec = pl.BlockSpec((None, tm, d), lambda i, j: (i, j, 0))
    return pl.pallas_call(
        _mix_out_kernel,
        grid=(b, t // tm),
        in_specs=[xspec, hspec, hspec, wspec,
                  pl.BlockSpec((None, 1, d), lambda i, j: (layer, 0, 0)),
                  wspec, mspec, mspec, wspec],
        out_specs=xspec,
        out_shape=jax.ShapeDtypeStruct((b, t, d), F32),
        scratch_shapes=[pltpu.VMEM((d, d), BF16)] * 3,
        compiler_params=_params(("arbitrary", "arbitrary")),
        name="mix_out",
    )(x, at, co, w_out, norm3, w_cq, mkb, mvb, w_co)


def _ffn_kernel(x_ref, g_ref, wa_ref, wg_ref, cwa_ref, cwg_ref, wd_ref, gf_ref,
                o_ref, hist_ref, hb_sc, acc_sc, buf, car_a, car_g, *, tm, final):
    t = pl.program_id(1)
    x = x_ref[...]
    hb_sc[...] = _rms(x, g_ref[...]).astype(BF16)
    acc_sc[...] = x

    @pl.when(t == 0)
    def _():
        car_a[...] = jnp.zeros(car_a.shape, F32)
        car_g[...] = jnp.zeros(car_g.shape, F32)

    def conv(u, stage, car, cw, c):
        stage[0:SUBLANES, :] = car[c]
        stage[SUBLANES:SUBLANES + tm, :] = u
        y = (cw[0:1] * stage[SUBLANES - 2:SUBLANES - 2 + tm, :]
             + cw[1:2] * stage[SUBLANES - 1:SUBLANES - 1 + tm, :]
             + cw[2:3] * u)
        car[c] = u[tm - SUBLANES:, :]
        return y

    def gated(c, slot):
        hb = hb_sc[...]
        a = conv(_dot(hb, wa_ref[c]), buf.at[2 * slot], car_a, cwa_ref[c], c)
        g = conv(_dot(hb, wg_ref[c]), buf.at[2 * slot + 1], car_g, cwg_ref[c], c)
        return (jax.nn.silu(g) * a).astype(BF16)

    def group(first, size):
        act = jnp.concatenate([gated(first + s, s) for s in range(size)], axis=1)
        acc_sc[...] += _dot(act, wd_ref[pl.ds(first, size)].reshape(size * FF_CHUNK, wd_ref.shape[2]))

    def trip(k, carry):
        group(FF_GROUP * k, FF_GROUP)
        return carry

    lax.fori_loop(0, N_FF_CHUNKS // FF_GROUP, trip, 0)
    if N_FF_CHUNKS % FF_GROUP:
        group(N_FF_CHUNKS - N_FF_CHUNKS % FF_GROUP, N_FF_CHUNKS % FF_GROUP)
    hist_ref[:, 0] = car_a[...]
    hist_ref[:, 1] = car_g[...]
    out = acc_sc[...]
    if final:
        out = _rms(out, gf_ref[...])
    o_ref[...] = out


def _ffn(x, norm3, wa, wg, cwa, cwg, wd, gf2, layer, final):
    b, t, d = x.shape
    tm = TM_FFN
    nch = N_FF_CHUNKS
    xspec = pl.BlockSpec((None, tm, d), lambda i, j: (i, j, 0))
    single = pl.Buffered(1)
    return pl.pallas_call(
        functools.partial(_ffn_kernel, tm=tm, final=final),
        grid=(b, t // tm),
        in_specs=[xspec,
                  pl.BlockSpec((None, 1, d), lambda i, j: (layer, 0, 0)),
                  pl.BlockSpec((None, nch, d, FF_CHUNK), lambda i, j: (layer, 0, 0, 0), pipeline_mode=single),
                  pl.BlockSpec((None, nch, d, FF_CHUNK), lambda i, j: (layer, 0, 0, 0), pipeline_mode=single),
                  pl.BlockSpec((None, nch, 3, FF_CHUNK), lambda i, j: (layer, 0, 0, 0)),
                  pl.BlockSpec((None, nch, 3, FF_CHUNK), lambda i, j: (layer, 0, 0, 0)),
                  pl.BlockSpec((None, nch, FF_CHUNK, d), lambda i, j: (layer, 0, 0, 0), pipeline_mode=single),
                  pl.BlockSpec((1, d), lambda i, j: (0, 0))],
        out_specs=[xspec,
                   pl.BlockSpec((None, nch, 2, SUBLANES, FF_CHUNK), lambda i, j: (i, 0, 0, 0, 0))],
        out_shape=[jax.ShapeDtypeStruct((b, t, d), F32),
                   jax.ShapeDtypeStruct((b, nch, 2, SUBLANES, FF_CHUNK), F32)],
        scratch_shapes=[pltpu.VMEM((tm, d), BF16),
                        pltpu.VMEM((tm, d), F32),
                        pltpu.VMEM((2 * FF_GROUP, tm + SUBLANES, FF_CHUNK), F32),
                        pltpu.VMEM((nch, SUBLANES, FF_CHUNK), F32),
                        pltpu.VMEM((nch, SUBLANES, FF_CHUNK), F32)],
        compiler_params=_params(("arbitrary", "arbitrary")),
        name="ffn",
    )(x, norm3, wa, wg, cwa, cwg, wd, gf2)


def _dec_in_kernel(x_ref, g_ref, w_ref, cw_ref, hist_ref, cos_ref, sa_ref, sb_ref,
                   q_ref, k_ref, v_ref, co_ref, nh_ref):
    h = _rms(x_ref[...], g_ref[...]).astype(BF16)
    cos, sa, sb = cos_ref[...], sa_ref[...], sb_ref[...]

    def proj(i):
        return _dot(h, w_ref[:, i * ATTN_WIDTH:(i + 1) * ATTN_WIDTH].astype(BF16))

    q_ref[...] = _rope(proj(0), cos, sa, sb) * (DIFF_HEAD_DIM ** -0.5)
    k_ref[...] = _rope(proj(1), cos, sa, sb)
    v_ref[...] = proj(2)
    gate_b = proj(3)
    cu = proj(4) * proj(5)
    h0 = hist_ref[:, 0:CONV_WIDTH]
    h1 = hist_ref[:, CONV_WIDTH:2 * CONV_WIDTH]
    cw = cw_ref[...]
    y = cw[0:1] * h0 + cw[1:2] * h1 + cw[2:3] * cu
    co_ref[...] = (gate_b * y).astype(BF16)
    nh_ref[:, 0:CONV_WIDTH] = h1
    nh_ref[:, CONV_WIDTH:2 * CONV_WIDTH] = cu


def _dec_in(x, norm3, w_in, conv_w, hist, tables, layer):
    n, d = x.shape
    cos, sa, sb = tables
    full = lambda shape: pl.BlockSpec(shape, lambda i: (0,) * len(shape))
    ospec = full((n, ATTN_WIDTH))
    return pl.pallas_call(
        _dec_in_kernel,
        grid=(1,),
        in_specs=[full((n, d)),
                  pl.BlockSpec((None, 1, d), lambda i: (layer, 0, 0)),
                  pl.BlockSpec((None, d, w_in.shape[2]), lambda i: (layer, 0, 0)),
                  pl.BlockSpec((None, 3, CONV_WIDTH), lambda i: (layer, 0, 0)),
                  pl.BlockSpec((None, n, 2 * CONV_WIDTH), lambda i: (layer, 0, 0)),
                  full((1, LANES)), full((1, LANES)), full((1, LANES))],
        out_specs=[ospec, ospec, ospec, ospec, full((n, 2 * CONV_WIDTH))],
        out_shape=[jax.ShapeDtypeStruct((n, ATTN_WIDTH), F32)] * 3
        + [jax.ShapeDtypeStruct((n, CONV_WIDTH), BF16),
           jax.ShapeDtypeStruct((n, 2 * CONV_WIDTH), F32)],
        compiler_params=_params(("arbitrary",)),
        name="dec_in",
    )(x, norm3, w_in, conv_w, hist, cos, sa, sb)


def _group_select(n_rows, width, group):
    r = lax.broadcasted_iota(jnp.int32, (n_rows, width), 0)
    c = lax.broadcasted_iota(jnp.int32, (n_rows, width), 1)
    return r == c // group


def _dec_attn_kernel(pt_ref, q_ref, kn_ref, vn_ref, lq1_ref, lk1_ref, lq2_ref, lk2_ref, sg_ref,
                     kt_hbm, v_hbm, o_ref, buf, sem, s_sc, p_sc, x_sc,
                     *, layer, lam_init, n_req, n_pages, page):
    cp = PAGES_PER_CHUNK
    nch = n_pages // cp
    width = ATTN_WIDTH
    n_combo = 2 * N_DIFF_HEADS
    jobs = 2 * nch

    def copies(b, job, slot):
        src = kt_hbm if job < nch else v_hbm
        c = job % nch
        return [pltpu.make_async_copy(src.at[layer, pt_ref[b, c * cp + g]], buf.at[slot, g], sem.at[slot])
                for g in range(cp)]

    def start(b, job, slot):
        for cpy in copies(b, job, slot):
            cpy.start()

    def wait(b, job, slot):
        for cpy in copies(b, job, slot):
            cpy.wait()

    lam = _lam(lq1_ref[...], lk1_ref[...], lq2_ref[...], lk2_ref[...], lam_init)
    sg = sg_ref[...]
    combo = lax.broadcasted_iota(jnp.int32, (n_combo, 1), 0)
    feat_sel = _group_select(n_combo, width, DIFF_HEAD_DIM)
    r = lax.broadcasted_iota(jnp.int32, (page, N_DIFF_HEADS * page), 0)
    c = lax.broadcasted_iota(jnp.int32, (page, N_DIFF_HEADS * page), 1)
    dilate = jnp.where(c // N_DIFF_HEADS == r, 1.0, 0.0).astype(BF16)
    rk = lax.broadcasted_iota(jnp.int32, (cp * n_combo, N_DIFF_HEADS * page), 0) % n_combo
    ch = lax.broadcasted_iota(jnp.int32, (cp * n_combo, N_DIFF_HEADS * page), 1) % N_DIFF_HEADS
    head_sel = rk // 2 == ch

    ahead = DEC_ATTN_SLOTS - 1

    def slot_of(b, job):
        return lax.rem(b * jobs + job, DEC_ATTN_SLOTS)

    for first in range(ahead):
        start(0, first, first)

    def request(b, carry):
        qrow = q_ref[b]
        qcol = jnp.broadcast_to(qrow, (page, width)).T
        vn = vn_ref[b]
        s_self = jnp.sum(jnp.where(feat_sel, jnp.broadcast_to(qrow * kn_ref[b], (n_combo, width)), 0.0),
                         axis=1, keepdims=True)
        o8 = None
        for job in range(jobs):
            slot = slot_of(b, job)
            nxt = job + ahead
            if nxt < jobs:
                start(b, nxt, slot_of(b, nxt))
            else:
                @pl.when(b + 1 < n_req)
                def _(nxt=nxt):
                    start(b + 1, nxt - jobs, slot_of(b + 1, nxt - jobs))
            wait(b, job, slot)
            if job < nch:
                def k_page(g, carry2, job=job, slot=slot):
                    prod = buf[slot, g] * qcol
                    s_sc[job * cp + g] = jnp.sum(prod.reshape(n_combo, DIFF_HEAD_DIM, page), axis=1)
                    return carry2

                lax.fori_loop(0, cp, k_page, 0)
                if job == nch - 1:
                    s = s_sc[...]
                    m = jnp.maximum(jnp.max(jnp.max(s, axis=0), axis=1, keepdims=True), s_self)
                    p = jnp.exp(s - m)
                    p_self = jnp.exp(s_self - m)
                    l = jnp.sum(jnp.sum(p, axis=0), axis=1, keepdims=True) + p_self
                    scale = jnp.where(combo % 2 == 0, 1.0, -lam) / l
                    p_sc[...] = p * scale
                    vn8 = jnp.concatenate(
                        [vn[:, (k // 2) * V_HEAD_DIM:(k // 2 + 1) * V_HEAD_DIM] for k in range(n_combo)], axis=0)
                    o8 = (p_self * scale) * vn8
            else:
                jv = job - nch
                pch = p_sc[jv * cp:(jv + 1) * cp].reshape(cp * n_combo, page).astype(BF16)
                x_sc[...] = jnp.where(head_sel, _dot(pch, dilate), 0.0)

                def v_page(g, acc, slot=slot):
                    xg = x_sc[pl.ds(pl.multiple_of(g * n_combo, n_combo), n_combo), :].astype(BF16)
                    return acc + _dot(xg, buf[slot, g].astype(BF16))

                o8 = lax.fori_loop(0, cp, v_page, o8, unroll=4)

        o_ref[b] = jnp.concatenate(
            [_rms(o8[2 * h:2 * h + 1] + o8[2 * h + 1:2 * h + 2], sg) * (1.0 - lam_init)
             for h in range(N_DIFF_HEADS)], axis=1)
        return carry

    lax.fori_loop(0, n_req, request, 0)


def _dec_attn(page_table, q, kn, vn, lams, sg3, cache_kt, cache_v2, layer, lam_init):
    n_req, n_pages = page_table.shape
    page = cache_kt.shape[3]
    cp = PAGES_PER_CHUNK
    n_combo = 2 * N_DIFF_HEADS
    vmem = pl.BlockSpec(memory_space=pltpu.VMEM)
    smem = pl.BlockSpec(memory_space=pltpu.SMEM)
    hbm = pl.BlockSpec(memory_space=pl.ANY)
    return pl.pallas_call(
        functools.partial(_dec_attn_kernel, layer=layer, lam_init=lam_init, n_req=n_req,
                          n_pages=n_pages, page=page),
        in_specs=[smem, vmem, vmem, vmem, vmem, vmem, vmem, vmem, vmem, hbm, hbm],
        out_specs=vmem,
        out_shape=jax.ShapeDtypeStruct((n_req, 1, ATTN_WIDTH), F32),
        scratch_shapes=[pltpu.VMEM((DEC_ATTN_SLOTS, cp, ATTN_WIDTH, page), F32),
                        pltpu.SemaphoreType.DMA((DEC_ATTN_SLOTS,)),
                        pltpu.VMEM((n_pages, n_combo, page), F32),
                        pltpu.VMEM((n_pages, n_combo, page), F32),
                        pltpu.VMEM((cp * n_combo, N_DIFF_HEADS * page), F32)],
        compiler_params=pltpu.CompilerParams(vmem_limit_bytes=VMEM_LIMIT_BYTES),
        name="dec_attn",
    )(page_table, *[a.reshape(n_req, 1, ATTN_WIDTH) for a in (q, kn, vn)],
      *[a[layer] for a in lams], sg3[layer], cache_kt, cache_v2).reshape(n_req, ATTN_WIDTH)


def _dec_mid_kernel(x_ref, at_ref, co_ref, wo_ref, g_ref, wq_ref, x1_ref, qc_ref):
    x1 = (x_ref[...] + _dot(at_ref[...].astype(BF16), wo_ref[0:ATTN_WIDTH, :].astype(BF16))
          + _dot(co_ref[...], wo_ref[ATTN_WIDTH:ATTN_WIDTH + CONV_WIDTH, :].astype(BF16)))
    x1_ref[...] = x1
    hq = _rms(x1, g_ref[...]).astype(BF16)
    qc_ref[...] = _dot(hq, wq_ref[...].astype(BF16)) * (MEM_HEAD_DIM ** -0.5)


def _dec_mid(x, at, co, w_out, norm3, w_cq, layer):
    n, d = x.shape
    full = lambda shape: pl.BlockSpec(shape, lambda i: (0,) * len(shape))
    wspec = pl.BlockSpec((None, d, d), lambda i: (layer, 0, 0))
    return pl.pallas_call(
        _dec_mid_kernel,
        grid=(1,),
        in_specs=[full((n, d)), full((n, ATTN_WIDTH)), full((n, CONV_WIDTH)), wspec,
                  pl.BlockSpec((None, 1, d), lambda i: (layer, 0, 0)), wspec],
        out_specs=[full((n, d)), full((n, d))],
        out_shape=[jax.ShapeDtypeStruct((n, d), F32)] * 2,
        compiler_params=_params(("arbitrary",)),
        name="dec_mid",
    )(x, at, co, w_out, norm3, w_cq)


def _dec_cross_kernel(q_ref, mk_ref, mv_ref, o_ref):
    tile = (mk_ref.shape[0] // SUBLANES, SUBLANES, LANES)
    r = jnp.sum(mk_ref[...].reshape(tile) * q_ref[...][None], axis=2, keepdims=True)
    s = r + pltpu.roll(r, N_MEM_HEADS, 1)
    m = jnp.max(s, axis=0, keepdims=True)
    p = jnp.exp(s - m)
    l = jnp.sum(p, axis=0, keepdims=True)
    o_ref[...] = jnp.sum((p / l) * mv_ref[...].reshape(tile), axis=0)


def _dec_cross(q8, mem_k, mem_v, layer):
    n = q8.shape[0]
    rows = mem_k.shape[2]
    qspec = pl.BlockSpec((None, SUBLANES, LANES), lambda i: (i, 0, 0))
    mspec = pl.BlockSpec((None, None, rows, LANES), lambda i: (layer, i, 0, 0))
    return pl.pallas_call(
        _dec_cross_kernel,
        grid=(n,),
        in_specs=[qspec, mspec, mspec],
        out_specs=qspec,
        out_shape=jax.ShapeDtypeStruct((n, SUBLANES, LANES), F32),
        compiler_params=_params(("arbitrary",)),
        name="dec_cross",
    )(q8, mem_k, mem_v)


def _dec_ffn_kernel(x1_ref, oc_ref, wco_ref, g_ref, wa_ref, wg_ref, cwa_ref, cwg_ref, wd_ref,
                    hist_ref, gf_ref, o_ref, nh_ref, *, final):
    x2 = x1_ref[...] + _dot(oc_ref[...].astype(BF16), wco_ref[...].astype(BF16))
    hb = _rms(x2, g_ref[...]).astype(BF16)
    acc = x2
    for c in range(N_FF_CHUNKS):
        ys = []
        for half, (w_ref, cw_ref) in enumerate(((wa_ref, cwa_ref), (wg_ref, cwg_ref))):
            lo = half * D_FF + c * FF_CHUNK
            u = _dot(hb, w_ref[c])
            h0 = hist_ref[:, lo:lo + FF_CHUNK]
            h1 = hist_ref[:, 2 * D_FF + lo:2 * D_FF + lo + FF_CHUNK]
            cw = cw_ref[c]
            ys.append(cw[0:1] * h0 + cw[1:2] * h1 + cw[2:3] * u)
            nh_ref[:, lo:lo + FF_CHUNK] = h1
            nh_ref[:, 2 * D_FF + lo:2 * D_FF + lo + FF_CHUNK] = u
        act = (jax.nn.silu(ys[1]) * ys[0]).astype(BF16)
        acc = acc + _dot(act, wd_ref[c])
    if final:
        acc = _rms(acc, gf_ref[...])
    o_ref[...] = acc


def _dec_ffn(x1, oc, w_co, norm3, wa, wg, cwa, cwg, wd, hist, gf2, layer, final):
    n, d = x1.shape
    nch = N_FF_CHUNKS
    full = lambda shape: pl.BlockSpec(shape, lambda i: (0,) * len(shape))
    return pl.pallas_call(
        functools.partial(_dec_ffn_kernel, final=final),
        grid=(1,),
        in_specs=[full((n, d)), full((n, d)),
                  pl.BlockSpec((None, d, d), lambda i: (layer, 0, 0)),
                  pl.BlockSpec((None, 1, d), lambda i: (layer, 0, 0)),
                  pl.BlockSpec((None, nch, d, FF_CHUNK), lambda i: (layer, 0, 0, 0)),
                  pl.BlockSpec((None, nch, d, FF_CHUNK), lambda i: (layer, 0, 0, 0)),
                  pl.BlockSpec((None, nch, 3, FF_CHUNK), lambda i: (layer, 0, 0, 0)),
                  pl.BlockSpec((None, nch, 3, FF_CHUNK), lambda i: (layer, 0, 0, 0)),
                  pl.BlockSpec((None, nch, FF_CHUNK, d), lambda i: (layer, 0, 0, 0)),
                  pl.BlockSpec((None, n, 4 * D_FF), lambda i: (layer, 0, 0)),
                  full((1, d))],
        out_specs=[full((n, d)), full((n, 4 * D_FF))],
        out_shape=[jax.ShapeDtypeStruct((n, d), F32), jax.ShapeDtypeStruct((n, 4 * D_FF), F32)],
        compiler_params=_params(("arbitrary",)),
        name="dec_ffn",
    )(x1, oc, w_co, norm3, wa, wg, cwa, cwg, wd, hist, gf2)


def kernel(x_prompt, x_sample, mem_prompt, cache_k, cache_v, state_conv, state_ffn_conv,
           cache_mem_k, cache_mem_v, page_table, w_in, w_out, conv_w, lambda_q1, lambda_k1,
           lambda_q2, lambda_k2, subln_g, norm_mix, norm_cross, norm_mem, w_cq, w_ck, w_cv,
           w_co, norm_ffn, w_up, ffn_conv_w, w_down, norm_final):
    depth = w_in.shape[0]
    bp, tp, d = x_prompt.shape
    bs, ts, _ = x_sample.shape
    n_pages = page_table.shape[1]
    page = cache_k.shape[2]
    past_len = n_pages * page
    n_phys = cache_k.shape[1]
    nch = N_FF_CHUNKS

    bf = lambda w: w.astype(BF16)
    chunked = lambda w: bf(w).reshape(depth, d, nch, FF_CHUNK).transpose(0, 2, 1, 3)
    wa_b = chunked(w_up[:, :, :D_FF])
    wg_b = chunked(w_up[:, :, D_FF:])
    wd_b = bf(w_down).reshape(depth, nch, FF_CHUNK, d)
    cw_chunked = lambda w: w.reshape(depth, 3, nch, FF_CHUNK).transpose(0, 2, 1, 3)
    cwa = cw_chunked(ffn_conv_w[:, :, :D_FF])
    cwg = cw_chunked(ffn_conv_w[:, :, D_FF:])
    row3 = lambda g: g.reshape(depth, 1, g.shape[-1])
    norm_mix3, norm_cross3, norm_mem3, norm_ffn3, subln3 = map(
        row3, (norm_mix, norm_cross, norm_mem, norm_ffn, subln_g))
    lams = tuple(map(row3, (lambda_q1, lambda_k1, lambda_q2, lambda_k2)))
    gf2 = norm_final.reshape(1, d)
    subln_col = subln_g.reshape(depth, V_HEAD_DIM, 1)

    tables_p = _rope_tables(jnp.arange(tp))
    tables_s = _rope_tables(past_len + jnp.arange(ts))

    cache_k4 = cache_k.transpose(0, 1, 3, 4, 5, 2).reshape(depth, n_phys, ATTN_WIDTH, page)
    cache_v4 = cache_v.reshape(depth, n_phys, N_DIFF_HEADS * page, V_HEAD_DIM)
    halves = MEM_HEAD_DIM // LANES
    mem_rows = lambda a: a.reshape(depth, bs, a.shape[2], N_MEM_HEADS, halves, LANES).transpose(
        0, 1, 2, 4, 3, 5).reshape(depth, bs, a.shape[2] * halves * N_MEM_HEADS, LANES)
    mem_k4 = mem_rows(cache_mem_k)
    mem_v4 = mem_rows(cache_mem_v)
    to_tile = lambda a: a.reshape(bs, N_MEM_HEADS, halves, LANES).transpose(0, 2, 1, 3).reshape(
        bs, halves * N_MEM_HEADS, LANES)
    from_tile = lambda a: a.reshape(bs, halves, N_MEM_HEADS, LANES).transpose(0, 2, 1, 3).reshape(bs, d)
    conv_hist = state_conv.reshape(depth, bs, 2 * CONV_WIDTH)
    ffn_hist = state_ffn_conv.reshape(depth, bs, 4 * D_FF)

    mk, mv, mkb, mvb = _memkv(mem_prompt, norm_mem3, w_ck, w_cv)

    xp = x_prompt
    xs = x_sample.reshape(bs * ts, d)
    kp_l, vp_l, cp_l, fp_l, ks_l, vs_l, cs_l, fs_l = ([] for _ in range(8))
    for l in range(depth):
        lam_init = 0.8 - 0.6 * math.exp(-0.3 * l)
        final = l == depth - 1
        q1, q2, kt, v4, kb, vt, co, chist = _mix_in(xp, norm_mix3, w_in, conv_w, tables_p, l)
        at = _attn(q1, q2, kb, vt, lams, subln_col, l, lam_init)
        x2 = _mix_out(xp, at, co, w_out, norm_cross3, w_cq, mkb, mvb, w_co, l)
        xp, fhist = _ffn(x2, norm_ffn3, wa_b, wg_b, cwa, cwg, wd_b, gf2, l, final)
        kp_l.append(kt)
        vp_l.append(v4)
        cp_l.append(chist[:, SUBLANES - 2:, :])
        fp_l.append(fhist[:, :, :, SUBLANES - 2:, :].transpose(0, 3, 2, 1, 4).reshape(bp, 2, 2 * D_FF))
        qd, kd, vd, cod, nconv = _dec_in(xs, norm_mix3, w_in, conv_w, conv_hist, tables_s, l)
        atd = _dec_attn(page_table, qd, kd, vd, lams, subln3, cache_k4, cache_v4, l, lam_init)
        x1d, qc = _dec_mid(xs, atd, cod, w_out, norm_cross3, w_cq, l)
        oc = from_tile(_dec_cross(to_tile(qc), mem_k4, mem_v4, l))
        xs, nffn = _dec_ffn(x1d, oc, w_co, norm_ffn3, wa_b, wg_b, cwa, cwg, wd_b, ffn_hist, gf2, l, final)
        ks_l.append(kd)
        vs_l.append(vd)
        cs_l.append(nconv)
        fs_l.append(nffn)

    st = jnp.stack
    return (xp,
            xs.reshape(bs, ts, d),
            st(kp_l).reshape(depth, bp, N_DIFF_HEADS, 2, DIFF_HEAD_DIM, tp).transpose(0, 1, 5, 2, 3, 4),
            st(vp_l).reshape(depth, bp, tp, N_DIFF_HEADS, V_HEAD_DIM),
            st(cp_l),
            st(fp_l),
            mk.reshape(depth, bp, -1, N_MEM_HEADS, MEM_HEAD_DIM),
            mv.reshape(depth, bp, -1, N_MEM_HEADS, MEM_HEAD_DIM),
            st(ks_l).reshape(depth, bs, ts, N_DIFF_HEADS, 2, DIFF_HEAD_DIM),
            st(vs_l).reshape(depth, bs, ts, N_DIFF_HEADS, V_HEAD_DIM),
            st(cs_l).reshape(depth, bs, 2, CONV_WIDTH),
            st(fs_l).reshape(depth, bs, 2, 2 * D_FF))
```

```python
import functools
import math

import jax
import jax.numpy as jnp
from jax import lax
from jax.experimental import pallas as pl
from jax.experimental.pallas import tpu as pltpu

F32 = jnp.float32
BF16 = jnp.bfloat16

D_MODEL = 1024
ATTN_WIDTH = 512
CONV_WIDTH = 512
N_DIFF_HEADS = 4
DIFF_HEAD_DIM = 64
V_HEAD_DIM = 2 * DIFF_HEAD_DIM
ROT_DIM = DIFF_HEAD_DIM // 4
ROPE_THETA = 500000.0
N_MEM_HEADS = 4
MEM_HEAD_DIM = D_MODEL // N_MEM_HEADS
D_FF = 11 * D_MODEL // 4
EPS = 1e-6
NEG_INF = -1e30
LOG2E = math.log2(math.e)

LANES = 128
SUBLANES = 8
VMEM_LIMIT_BYTES = 56 * 1024 * 1024

TM_MIX = 1024
TQ = 512
TM_FFN = 1024
FF_CHUNK = 256
N_FF_CHUNKS = D_FF // FF_CHUNK
FF_GROUP = 4
PAGES_PER_CHUNK = 16
DEC_ATTN_SLOTS = 3


def _dot(a, b):
    return jnp.dot(a, b, preferred_element_type=F32)


def _dot_nt(a, b):
    return lax.dot_general(a, b, (((1,), (1,)), ((), ())), preferred_element_type=F32)


def _rms(x, g):
    ms = jnp.mean(x * x, axis=-1, keepdims=True)
    return x * lax.rsqrt(ms + EPS) * g


def _rope(y, cos, sa, sb):
    outs = []
    for c in range(y.shape[1] // LANES):
        yc = y[:, c * LANES:(c + 1) * LANES]
        outs.append(yc * cos + pltpu.roll(yc, ROT_DIM // 2, 1) * sa
                    + pltpu.roll(yc, LANES - ROT_DIM // 2, 1) * sb)
    return jnp.concatenate(outs, axis=1)


def _rope_tables(pos):
    half = ROT_DIM // 2
    inv = jnp.power(ROPE_THETA, -jnp.arange(half, dtype=F32) * 2.0 / ROT_DIM)
    ang = pos.astype(F32)[:, None] * inv[None, :]
    cos, sin = jnp.cos(ang), jnp.sin(ang)
    n = pos.shape[0]
    ones = jnp.ones((n, DIFF_HEAD_DIM - ROT_DIM), F32)
    zeros = jnp.zeros_like(ones)
    z8 = jnp.zeros((n, half), F32)
    c64 = jnp.concatenate([cos, cos, ones], axis=1)
    sa64 = jnp.concatenate([z8, sin, zeros], axis=1)
    sb64 = jnp.concatenate([-sin, z8, zeros], axis=1)
    tile2 = lambda a: jnp.concatenate([a, a], axis=1)
    return tile2(c64), tile2(sa64), tile2(sb64)


def _lam(lq1, lk1, lq2, lk2, lam_init):
    return (jnp.exp(jnp.sum(lq1 * lk1, axis=-1, keepdims=True))
            - jnp.exp(jnp.sum(lq2 * lk2, axis=-1, keepdims=True)) + lam_init)


def _params(sem=None):
    return pltpu.CompilerParams(dimension_semantics=sem, vmem_limit_bytes=VMEM_LIMIT_BYTES)


def _cast_weights(first, pairs):
    @pl.when(first)
    def _():
        for src, dst in pairs:
            dst[...] = src[...].astype(BF16)


def _memkv_kernel(mem_ref, g_ref, wk_ref, wv_ref, mk_ref, mv_ref, mkb_ref, mvb_ref, wkb, wvb):
    _cast_weights(pl.program_id(1) == 0, ((wk_ref, wkb), (wv_ref, wvb)))
    hm = _rms(mem_ref[...], g_ref[...]).astype(BF16)
    mk = _dot(hm, wkb[...])
    mv = _dot(hm, wvb[...])
    mk_ref[...] = mk
    mv_ref[...] = mv
    mkb_ref[...] = mk.astype(BF16)
    mvb_ref[...] = mv.astype(BF16)


def _memkv(mem, g3, wk, wv):
    depth = wk.shape[0]
    b, n, d = mem.shape
    wspec = pl.BlockSpec((None, d, d), lambda l, i: (l, 0, 0))
    ospec = pl.BlockSpec((None, None, n, d), lambda l, i: (l, i, 0, 0))
    return pl.pallas_call(
        _memkv_kernel,
        grid=(depth, b),
        in_specs=[pl.BlockSpec((None, n, d), lambda l, i: (i, 0, 0)),
                  pl.BlockSpec((None, 1, d), lambda l, i: (l, 0, 0)),
                  wspec, wspec],
        out_specs=[ospec, ospec, ospec, ospec],
        out_shape=[jax.ShapeDtypeStruct((depth, b, n, d), F32)] * 2
        + [jax.ShapeDtypeStruct((depth, b, n, d), BF16)] * 2,
        scratch_shapes=[pltpu.VMEM((d, d), BF16)] * 2,
        compiler_params=_params(("arbitrary", "arbitrary")),
        name="memkv",
    )(mem, g3, wk, wv)


def _mix_in_kernel(x_ref, g_ref, w_ref, cw_ref, cos_ref, sa_ref, sb_ref, kt_all_ref, v_all_ref,
                   q1_ref, q2_ref, kt_ref, v_ref, kb_ref, vt_ref, co_ref, hist_ref,
                   cbuf, wb, *, tm):
    del kt_all_ref, v_all_ref
    t = pl.program_id(1)
    _cast_weights((pl.program_id(0) == 0) & (t == 0), ((w_ref, wb),))
    h = _rms(x_ref[...], g_ref[...]).astype(BF16)
    cos, sa, sb = cos_ref[...], sa_ref[...], sb_ref[...]

    def proj(i):
        return _dot(h, wb[:, i * ATTN_WIDTH:(i + 1) * ATTN_WIDTH])

    lane = lax.broadcasted_iota(jnp.int32, (tm, ATTN_WIDTH), 1)
    first_map = (lane % V_HEAD_DIM) < DIFF_HEAD_DIM
    q = _rope(proj(0), cos, sa, sb) * (DIFF_HEAD_DIM ** -0.5 * LOG2E)
    q1_ref[...] = jnp.where(first_map, q, 0.0).astype(BF16)
    q2_ref[...] = jnp.where(first_map, 0.0, q).astype(BF16)
    k = _rope(proj(1), cos, sa, sb)
    kt_ref[...] = k.T
    kb_ref[...] = k.astype(BF16)
    v = proj(2)
    for hd in range(N_DIFF_HEADS):
        v_ref[pl.ds(hd, tm, stride=N_DIFF_HEADS), :] = v[:, hd * V_HEAD_DIM:(hd + 1) * V_HEAD_DIM]
    vt_ref[...] = v.T.astype(BF16)

    gate_b = proj(3)
    cu = proj(4) * proj(5)

    @pl.when(t == 0)
    def _():
        cbuf[0:SUBLANES, :] = jnp.zeros((SUBLANES, CONV_WIDTH), F32)

    cbuf[SUBLANES:SUBLANES + tm, :] = cu
    cw = cw_ref[...]
    y = (cw[0:1] * cbuf[SUBLANES - 2:SUBLANES - 2 + tm, :]
         + cw[1:2] * cbuf[SUBLANES - 1:SUBLANES - 1 + tm, :]
         + cw[2:3] * cu)
    co_ref[...] = (gate_b * y).astype(BF16)
    tail = cu[tm - SUBLANES:, :]
    hist_ref[...] = tail
    cbuf[0:SUBLANES, :] = tail


def _mix_in(x, norm3, w_in, conv_w, tables, kt_all, v_all, layer):
    b, t, d = x.shape
    tm = TM_MIX
    cos, sa, sb = tables
    tok = lambda w, dt: jax.ShapeDtypeStruct((b, t, w), dt)
    tspec = pl.BlockSpec((tm, LANES), lambda i, j: (j, 0))
    ospec = pl.BlockSpec((None, tm, ATTN_WIDTH), lambda i, j: (i, j, 0))
    tspec_t = pl.BlockSpec((None, ATTN_WIDTH, tm), lambda i, j: (i, 0, j))
    hbm = pl.BlockSpec(memory_space=pl.ANY)
    return pl.pallas_call(
        functools.partial(_mix_in_kernel, tm=tm),
        grid=(b, t // tm),
        in_specs=[pl.BlockSpec((None, tm, d), lambda i, j: (i, j, 0)),
                  pl.BlockSpec((None, 1, d), lambda i, j: (layer, 0, 0)),
                  pl.BlockSpec((None, d, w_in.shape[2]), lambda i, j: (layer, 0, 0), pipeline_mode=pl.Buffered(1)),
                  pl.BlockSpec((None, 3, CONV_WIDTH), lambda i, j: (layer, 0, 0)),
                  tspec, tspec, tspec, hbm, hbm],
        out_specs=[ospec, ospec,
                   pl.BlockSpec((None, None, ATTN_WIDTH, tm), lambda i, j: (layer, i, 0, j)),
                   pl.BlockSpec((None, None, N_DIFF_HEADS * tm, V_HEAD_DIM), lambda i, j: (layer, i, j, 0)),
                   ospec, tspec_t, ospec,
                   pl.BlockSpec((None, SUBLANES, CONV_WIDTH), lambda i, j: (i, 0, 0))],
        out_shape=[tok(ATTN_WIDTH, BF16), tok(ATTN_WIDTH, BF16),
                   jax.ShapeDtypeStruct(kt_all.shape, F32),
                   jax.ShapeDtypeStruct(v_all.shape, F32),
                   tok(ATTN_WIDTH, BF16), jax.ShapeDtypeStruct((b, ATTN_WIDTH, t), BF16), tok(CONV_WIDTH, BF16),
                   jax.ShapeDtypeStruct((b, SUBLANES, CONV_WIDTH), F32)],
        input_output_aliases={7: 2, 8: 3},
        scratch_shapes=[pltpu.VMEM((tm + SUBLANES, CONV_WIDTH), F32),
                        pltpu.VMEM((d, w_in.shape[2]), BF16)],
        compiler_params=_params(("arbitrary", "arbitrary")),
        name="mix_in",
    )(x, norm3, w_in, conv_w, cos, sa, sb, kt_all, v_all)


def _attn_kernel(qi_ref, kj_ref, q1_ref, q2_ref, k_ref, vt_ref, lq1_ref, lk1_ref, lq2_ref, lk2_ref, sg_ref,
                 o_ref, s_sc, m_sc, l_sc, acc_sc, *, tq, lam_init):
    i = qi_ref[pl.program_id(1)]
    j = kj_ref[pl.program_id(1)]
    ones_rows = 16

    @pl.when(j == 0)
    def _():
        m_sc[...] = jnp.full(m_sc.shape, NEG_INF, F32)
        l_sc[...] = jnp.zeros(l_sc.shape, F32)
        acc_sc[...] = jnp.zeros(acc_sc.shape, F32)

    def step(masked):
        for h in range(N_DIFF_HEADS):
            hs = slice(h * V_HEAD_DIM, (h + 1) * V_HEAD_DIM)
            kh = k_ref[:, hs]
            for c, q_ref in enumerate((q1_ref, q2_ref)):
                s_sc[2 * h + c] = _dot_nt(kh, q_ref[:, hs])
        s = s_sc[...]
        if masked:
            key = lax.broadcasted_iota(jnp.int32, (tq, tq), 0)
            qry = lax.broadcasted_iota(jnp.int32, (tq, tq), 1)
            s = jnp.where((key <= qry)[None], s, NEG_INF)
        m_old = m_sc[...]
        m_new = jnp.maximum(m_old, jnp.max(s, axis=1, keepdims=True))
        alpha = jnp.exp2(m_old - m_new)
        p = jnp.exp2(s - m_new).astype(BF16)
        m_sc[...] = m_new
        ones = jnp.ones((ones_rows, tq), BF16)
        for h in range(N_DIFF_HEADS):
            lhs = jnp.concatenate([vt_ref[h * V_HEAD_DIM:(h + 1) * V_HEAD_DIM, :], ones], axis=0)
            for c in range(2):
                idx = 2 * h + c
                r = _dot(lhs, p[idx])
                acc_sc[idx] = alpha[idx] * acc_sc[idx] + r[0:V_HEAD_DIM]
                l_sc[idx] = alpha[idx] * l_sc[idx] + r[V_HEAD_DIM:V_HEAD_DIM + 1]

    @pl.when(j < i)
    def _():
        step(False)

    @pl.when(j == i)
    def _():
        step(True)
        lam = _lam(lq1_ref[...], lk1_ref[...], lq2_ref[...], lk2_ref[...], lam_init)
        sg = sg_ref[...]
        for h in range(N_DIFF_HEADS):
            o = acc_sc[2 * h] / l_sc[2 * h] - lam * (acc_sc[2 * h + 1] / l_sc[2 * h + 1])
            ms = jnp.mean(o * o, axis=0, keepdims=True)
            o = o * lax.rsqrt(ms + EPS) * sg * (1.0 - lam_init)
            o_ref[:, h * V_HEAD_DIM:(h + 1) * V_HEAD_DIM] = o.T.astype(BF16)


def _attn(q1, q2, kb, vt, lams, sg_col, layer, lam_init):
    b, t, w = q1.shape
    tq = TQ
    nq = t // tq
    n_combo = 2 * N_DIFF_HEADS
    pairs = [(i, j) for i in range(nq) for j in range(i + 1)]
    qi = jnp.asarray([p[0] for p in pairs], jnp.int32)
    kj = jnp.asarray([p[1] for p in pairs], jnp.int32)
    qspec = pl.BlockSpec((None, tq, w), lambda bi, s, qi, kj: (bi, qi[s], 0))
    kspec = pl.BlockSpec((None, tq, w), lambda bi, s, qi, kj: (bi, kj[s], 0))
    vspec = pl.BlockSpec((None, w, tq), lambda bi, s, qi, kj: (bi, 0, kj[s]))
    lspec = pl.BlockSpec((None, 1, DIFF_HEAD_DIM), lambda bi, s, qi, kj: (layer, 0, 0))
    return pl.pallas_call(
        functools.partial(_attn_kernel, tq=tq, lam_init=lam_init),
        grid_spec=pltpu.PrefetchScalarGridSpec(
            num_scalar_prefetch=2,
            grid=(b, len(pairs)),
            in_specs=[qspec, qspec, kspec, vspec, lspec, lspec, lspec, lspec,
                      pl.BlockSpec((None, V_HEAD_DIM, 1), lambda bi, s, qi, kj: (layer, 0, 0))],
            out_specs=pl.BlockSpec((None, tq, w), lambda bi, s, qi, kj: (bi, qi[s], 0)),
            scratch_shapes=[pltpu.VMEM((n_combo, tq, tq), F32),
                            pltpu.VMEM((n_combo, 1, tq), F32),
                            pltpu.VMEM((n_combo, 1, tq), F32),
                            pltpu.VMEM((n_combo, V_HEAD_DIM, tq), F32)]),
        out_shape=jax.ShapeDtypeStruct((b, t, w), BF16),
        compiler_params=_params(("arbitrary", "arbitrary")),
        name="diff_attn",
    )(qi, kj, q1, q2, kb, vt, *lams, sg_col)


def _cross_attend(qc, mk_ref, mv_ref):
    outs = []
    for h in range(N_MEM_HEADS):
        hs = slice(h * MEM_HEAD_DIM, (h + 1) * MEM_HEAD_DIM)
        s = _dot_nt(qc[:, hs], mk_ref[:, hs])
        m = jnp.max(s, axis=1, keepdims=True)
        p = jnp.exp(s - m)
        l = jnp.sum(p, axis=1, keepdims=True)
        outs.append((_dot(p.astype(BF16), mv_ref[:, hs]) / l).astype(BF16))
    return jnp.concatenate(outs, axis=1)


def _mix_out_kernel(x_ref, at_ref, co_ref, wo_ref, g_ref, wq_ref, mk_ref, mv_ref, wco_ref, o_ref,
                    wob, wqb, wcob):
    _cast_weights((pl.program_id(0) == 0) & (pl.program_id(1) == 0),
                  ((wo_ref, wob), (wq_ref, wqb), (wco_ref, wcob)))
    x1 = x_ref[...] + _dot(jnp.concatenate([at_ref[...], co_ref[...]], axis=1), wob[...])
    hq = _rms(x1, g_ref[...]).astype(BF16)
    qc = (_dot(hq, wqb[...]) * (MEM_HEAD_DIM ** -0.5)).astype(BF16)
    o = _cross_attend(qc, mk_ref, mv_ref)
    o_ref[...] = x1 + _dot(o, wcob[...])


def _mix_out(x, at, co, w_out, norm3, w_cq, mkb, mvb, w_co, layer):
    b, t, d = x.shape
    tm = TM_MIX
    n_mem = mkb.shape[2]
    wspec = pl.BlockSpec((None, d, d), lambda i, j: (layer, 0, 0), pipeline_mode=pl.Buffered(1))
    hspec = pl.BlockSpec((None, tm, ATTN_WIDTH), lambda i, j: (i, j, 0))
    mspec = pl.BlockSpec((None, None, n_mem, d), lambda i, j: (layer, i, 0, 0))
    xspec = pl.BlockSpec((None, tm, d), lambda i, j: (i, j, 0))
    return pl.pallas_call(
        _mix_out_kernel,
        grid=(b, t // tm),
        in_specs=[xspec, hspec, hspec, wspec,
                  pl.BlockSpec((None, 1, d), lambda i, j: (layer, 0, 0)),
                  wspec, mspec, mspec, wspec],
        out_specs=xspec,
        out_shape=jax.ShapeDtypeStruct((b, t, d), F32),
        scratch_shapes=[pltpu.VMEM((d, d), BF16)] * 3,
        compiler_params=_params(("arbitrary", "arbitrary")),
        name="mix_out",
    )(x, at, co, w_out, norm3, w_cq, mkb, mvb, w_co)


def _ffn_kernel(x_ref, g_ref, wa_ref, wg_ref, cwa_ref, cwg_ref, wd_ref, gf_ref,
                o_ref, hist_ref, hb_sc, acc_sc, buf, car_a, car_g, *, tm, final):
    t = pl.program_id(1)
    x = x_ref[...]
    hb_sc[...] = _rms(x, g_ref[...]).astype(BF16)
    acc_sc[...] = x

    @pl.when(t == 0)
    def _():
        car_a[...] = jnp.zeros(car_a.shape, F32)
        car_g[...] = jnp.zeros(car_g.shape, F32)

    def conv(u, stage, car, cw, c):
        stage[0:SUBLANES, :] = car[c]
        stage[SUBLANES:SUBLANES + tm, :] = u
        y = (cw[0:1] * stage[SUBLANES - 2:SUBLANES - 2 + tm, :]
             + cw[1:2] * stage[SUBLANES - 1:SUBLANES - 1 + tm, :]
             + cw[2:3] * u)
        car[c] = u[tm - SUBLANES:, :]
        return y

    def gated(c, slot):
        hb = hb_sc[...]
        a = conv(_dot(hb, wa_ref[c]), buf.at[2 * slot], car_a, cwa_ref[c], c)
        g = conv(_dot(hb, wg_ref[c]), buf.at[2 * slot + 1], car_g, cwg_ref[c], c)
        return (jax.nn.silu(g) * a).astype(BF16)

    def group(first, size):
        act = jnp.concatenate([gated(first + s, s) for s in range(size)], axis=1)
        acc_sc[...] += _dot(act, wd_ref[pl.ds(first, size)].reshape(size * FF_CHUNK, wd_ref.shape[2]))

    def trip(k, carry):
        group(FF_GROUP * k, FF_GROUP)
        return carry

    lax.fori_loop(0, N_FF_CHUNKS // FF_GROUP, trip, 0)
    if N_FF_CHUNKS % FF_GROUP:
        group(N_FF_CHUNKS - N_FF_CHUNKS % FF_GROUP, N_FF_CHUNKS % FF_GROUP)
    hist_ref[:, 0] = car_a[...]
    hist_ref[:, 1] = car_g[...]
    out = acc_sc[...]
    if final:
        out = _rms(out, gf_ref[...])
    o_ref[...] = out


def _ffn(x, norm3, wa, wg, cwa, cwg, wd, gf2, layer, final):
    b, t, d = x.shape
    tm = TM_FFN
    nch = N_FF_CHUNKS
    xspec = pl.BlockSpec((None, tm, d), lambda i, j: (i, j, 0))
    single = pl.Buffered(1)
    return pl.pallas_call(
        functools.partial(_ffn_kernel, tm=tm, final=final),
        grid=(b, t // tm),
        in_specs=[xspec,
                  pl.BlockSpec((None, 1, d), lambda i, j: (layer, 0, 0)),
                  pl.BlockSpec((None, nch, d, FF_CHUNK), lambda i, j: (layer, 0, 0, 0), pipeline_mode=single),
                  pl.BlockSpec((None, nch, d, FF_CHUNK), lambda i, j: (layer, 0, 0, 0), pipeline_mode=single),
                  pl.BlockSpec((None, nch, 3, FF_CHUNK), lambda i, j: (layer, 0, 0, 0)),
                  pl.BlockSpec((None, nch, 3, FF_CHUNK), lambda i, j: (layer, 0, 0, 0)),
                  pl.BlockSpec((None, nch, FF_CHUNK, d), lambda i, j: (layer, 0, 0, 0), pipeline_mode=single),
                  pl.BlockSpec((1, d), lambda i, j: (0, 0))],
        out_specs=[xspec,
                   pl.BlockSpec((None, nch, 2, SUBLANES, FF_CHUNK), lambda i, j: (i, 0, 0, 0, 0))],
        out_shape=[jax.ShapeDtypeStruct((b, t, d), F32),
                   jax.ShapeDtypeStruct((b, nch, 2, SUBLANES, FF_CHUNK), F32)],
        scratch_shapes=[pltpu.VMEM((tm, d), BF16),
                        pltpu.VMEM((tm, d), F32),
                        pltpu.VMEM((2 * FF_GROUP, tm + SUBLANES, FF_CHUNK), F32),
                        pltpu.VMEM((nch, SUBLANES, FF_CHUNK), F32),
                        pltpu.VMEM((nch, SUBLANES, FF_CHUNK), F32)],
        compiler_params=_params(("arbitrary", "arbitrary")),
        name="ffn",
    )(x, norm3, wa, wg, cwa, cwg, wd, gf2)


def _dec_in_kernel(x_ref, g_ref, w_ref, cw_ref, hist_ref, cos_ref, sa_ref, sb_ref,
                   q_ref, k_ref, v_ref, co_ref, nh_ref):
    h = _rms(x_ref[...], g_ref[...]).astype(BF16)
    cos, sa, sb = cos_ref[...], sa_ref[...], sb_ref[...]

    def proj(i):
        return _dot(h, w_ref[:, i * ATTN_WIDTH:(i + 1) * ATTN_WIDTH].astype(BF16))

    q_ref[...] = _rope(proj(0), cos, sa, sb) * (DIFF_HEAD_DIM ** -0.5)
    k_ref[...] = _rope(proj(1), cos, sa, sb)
    v_ref[...] = proj(2)
    gate_b = proj(3)
    cu = proj(4) * proj(5)
    h0 = hist_ref[:, 0:CONV_WIDTH]
    h1 = hist_ref[:, CONV_WIDTH:2 * CONV_WIDTH]
    cw = cw_ref[...]
    y = cw[0:1] * h0 + cw[1:2] * h1 + cw[2:3] * cu
    co_ref[...] = (gate_b * y).astype(BF16)
    nh_ref[:, 0:CONV_WIDTH] = h1
    nh_ref[:, CONV_WIDTH:2 * CONV_WIDTH] = cu


def _dec_in(x, norm3, w_in, conv_w, hist, tables, layer):
    n, d = x.shape
    cos, sa, sb = tables
    full = lambda shape: pl.BlockSpec(shape, lambda i: (0,) * len(shape))
    ospec = full((n, ATTN_WIDTH))
    return pl.pallas_call(
        _dec_in_kernel,
        grid=(1,),
        in_specs=[full((n, d)),
                  pl.BlockSpec((None, 1, d), lambda i: (layer, 0, 0)),
                  pl.BlockSpec((None, d, w_in.shape[2]), lambda i: (layer, 0, 0)),
                  pl.BlockSpec((None, 3, CONV_WIDTH), lambda i: (layer, 0, 0)),
                  pl.BlockSpec((None, n, 2 * CONV_WIDTH), lambda i: (layer, 0, 0)),
                  full((1, LANES)), full((1, LANES)), full((1, LANES))],
        out_specs=[ospec, ospec, ospec, ospec, full((n, 2 * CONV_WIDTH))],
        out_shape=[jax.ShapeDtypeStruct((n, ATTN_WIDTH), F32)] * 3
        + [jax.ShapeDtypeStruct((n, CONV_WIDTH), BF16),
           jax.ShapeDtypeStruct((n, 2 * CONV_WIDTH), F32)],
        compiler_params=_params(("arbitrary",)),
        name="dec_in",
    )(x, norm3, w_in, conv_w, hist, cos, sa, sb)


def _group_select(n_rows, width, group):
    r = lax.broadcasted_iota(jnp.int32, (n_rows, width), 0)
    c = lax.broadcasted_iota(jnp.int32, (n_rows, width), 1)
    return r == c // group


def _dec_attn_kernel(pt_ref, q_ref, kn_ref, vn_ref, lq1_ref, lk1_ref, lq2_ref, lk2_ref, sg_ref,
                     kt_hbm, v_hbm, o_ref, buf, sem, s_sc, p_sc, x_sc,
                     *, layer, lam_init, n_req, n_pages, page):
    cp = PAGES_PER_CHUNK
    nch = n_pages // cp
    width = ATTN_WIDTH
    n_combo = 2 * N_DIFF_HEADS
    jobs = 2 * nch

    def copies(b, job, slot):
        src = kt_hbm if job < nch else v_hbm
        c = job % nch
        return [pltpu.make_async_copy(src.at[layer, pt_ref[b, c * cp + g]], buf.at[slot, g], sem.at[slot])
                for g in range(cp)]

    def start(b, job, slot):
        for cpy in copies(b, job, slot):
            cpy.start()

    def wait(b, job, slot):
        for cpy in copies(b, job, slot):
            cpy.wait()

    lam = _lam(lq1_ref[...], lk1_ref[...], lq2_ref[...], lk2_ref[...], lam_init)
    sg = sg_ref[...]
    combo = lax.broadcasted_iota(jnp.int32, (n_combo, 1), 0)
    feat_sel = _group_select(n_combo, width, DIFF_HEAD_DIM)
    r = lax.broadcasted_iota(jnp.int32, (page, N_DIFF_HEADS * page), 0)
    c = lax.broadcasted_iota(jnp.int32, (page, N_DIFF_HEADS * page), 1)
    dilate = jnp.where(c // N_DIFF_HEADS == r, 1.0, 0.0).astype(BF16)
    rk = lax.broadcasted_iota(jnp.int32, (cp * n_combo, N_DIFF_HEADS * page), 0) % n_combo
    ch = lax.broadcasted_iota(jnp.int32, (cp * n_combo, N_DIFF_HEADS * page), 1) % N_DIFF_HEADS
    head_sel = rk // 2 == ch

    ahead = DEC_ATTN_SLOTS - 1

    def slot_of(b, job):
        return lax.rem(b * jobs + job, DEC_ATTN_SLOTS)

    for first in range(ahead):
        start(0, first, first)

    def request(b, carry):
        qrow = q_ref[b]
        qcol = jnp.broadcast_to(qrow, (page, width)).T
        vn = vn_ref[b]
        s_self = jnp.sum(jnp.where(feat_sel, jnp.broadcast_to(qrow * kn_ref[b], (n_combo, width)), 0.0),
                         axis=1, keepdims=True)
        o8 = None
        for job in range(jobs):
            slot = slot_of(b, job)
            nxt = job + ahead
            if nxt < jobs:
                start(b, nxt, slot_of(b, nxt))
            else:
                @pl.when(b + 1 < n_req)
                def _(nxt=nxt):
                    start(b + 1, nxt - jobs, slot_of(b + 1, nxt - jobs))
            wait(b, job, slot)
            if job < nch:
                def k_page(g, carry2, job=job, slot=slot):
                    prod = buf[slot, g] * qcol
                    s_sc[job * cp + g] = jnp.sum(prod.reshape(n_combo, DIFF_HEAD_DIM, page), axis=1)
                    return carry2

                lax.fori_loop(0, cp, k_page, 0)
                if job == nch - 1:
                    s = s_sc[...]
                    m = jnp.maximum(jnp.max(jnp.max(s, axis=0), axis=1, keepdims=True), s_self)
                    p = jnp.exp(s - m)
                    p_self = jnp.exp(s_self - m)
                    l = jnp.sum(jnp.sum(p, axis=0), axis=1, keepdims=True) + p_self
                    scale = jnp.where(combo % 2 == 0, 1.0, -lam) / l
                    p_sc[...] = p * scale
                    vn8 = jnp.concatenate(
                        [vn[:, (k // 2) * V_HEAD_DIM:(k // 2 + 1) * V_HEAD_DIM] for k in range(n_combo)], axis=0)
                    o8 = (p_self * scale) * vn8
            else:
                jv = job - nch
                pch = p_sc[jv * cp:(jv + 1) * cp].reshape(cp * n_combo, page).astype(BF16)
                x_sc[...] = jnp.where(head_sel, _dot(pch, dilate), 0.0)

                def v_page(g, acc, slot=slot):
                    xg = x_sc[pl.ds(pl.multiple_of(g * n_combo, n_combo), n_combo), :].astype(BF16)
                    return acc + _dot(xg, buf[slot, g].astype(BF16))

                o8 = lax.fori_loop(0, cp, v_page, o8, unroll=4)

        o_ref[b] = jnp.concatenate(
            [_rms(o8[2 * h:2 * h + 1] + o8[2 * h + 1:2 * h + 2], sg) * (1.0 - lam_init)
             for h in range(N_DIFF_HEADS)], axis=1)
        return carry

    lax.fori_loop(0, n_req, request, 0)


def _dec_attn(page_table, q, kn, vn, lams, sg3, cache_kt, cache_v2, layer, lam_init):
    n_req, n_pages = page_table.shape
    page = cache_kt.shape[3]
    cp = PAGES_PER_CHUNK
    n_combo = 2 * N_DIFF_HEADS
    vmem = pl.BlockSpec(memory_space=pltpu.VMEM)
    smem = pl.BlockSpec(memory_space=pltpu.SMEM)
    hbm = pl.BlockSpec(memory_space=pl.ANY)
    return pl.pallas_call(
        functools.partial(_dec_attn_kernel, layer=layer, lam_init=lam_init, n_req=n_req,
                          n_pages=n_pages, page=page),
        in_specs=[smem, vmem, vmem, vmem, vmem, vmem, vmem, vmem, vmem, hbm, hbm],
        out_specs=vmem,
        out_shape=jax.ShapeDtypeStruct((n_req, 1, ATTN_WIDTH), F32),
        scratch_shapes=[pltpu.VMEM((DEC_ATTN_SLOTS, cp, ATTN_WIDTH, page), F32),
                        pltpu.SemaphoreType.DMA((DEC_ATTN_SLOTS,)),
                        pltpu.VMEM((n_pages, n_combo, page), F32),
                        pltpu.VMEM((n_pages, n_combo, page), F32),
                        pltpu.VMEM((cp * n_combo, N_DIFF_HEADS * page), F32)],
        compiler_params=pltpu.CompilerParams(vmem_limit_bytes=VMEM_LIMIT_BYTES),
        name="dec_attn",
    )(page_table, *[a.reshape(n_req, 1, ATTN_WIDTH) for a in (q, kn, vn)],
      *[a[layer] for a in lams], sg3[layer], cache_kt, cache_v2).reshape(n_req, ATTN_WIDTH)


def _dec_mid_kernel(x_ref, at_ref, co_ref, wo_ref, g_ref, wq_ref, x1_ref, qc_ref):
    x1 = (x_ref[...] + _dot(at_ref[...].astype(BF16), wo_ref[0:ATTN_WIDTH, :].astype(BF16))
          + _dot(co_ref[...], wo_ref[ATTN_WIDTH:ATTN_WIDTH + CONV_WIDTH, :].astype(BF16)))
    x1_ref[...] = x1
    hq = _rms(x1, g_ref[...]).astype(BF16)
    qc_ref[...] = _dot(hq, wq_ref[...].astype(BF16)) * (MEM_HEAD_DIM ** -0.5)


def _dec_mid(x, at, co, w_out, norm3, w_cq, layer):
    n, d = x.shape
    full = lambda shape: pl.BlockSpec(shape, lambda i: (0,) * len(shape))
    wspec = pl.BlockSpec((None, d, d), lambda i: (layer, 0, 0))
    return pl.pallas_call(
        _dec_mid_kernel,
        grid=(1,),
        in_specs=[full((n, d)), full((n, ATTN_WIDTH)), full((n, CONV_WIDTH)), wspec,
                  pl.BlockSpec((None, 1, d), lambda i: (layer, 0, 0)), wspec],
        out_specs=[full((n, d)), full((n, d))],
        out_shape=[jax.ShapeDtypeStruct((n, d), F32)] * 2,
        compiler_params=_params(("arbitrary",)),
        name="dec_mid",
    )(x, at, co, w_out, norm3, w_cq)


def _dec_cross_kernel(q_ref, mk_ref, mv_ref, o_ref):
    tile = (mk_ref.shape[0] // SUBLANES, SUBLANES, LANES)
    r = jnp.sum(mk_ref[...].reshape(tile) * q_ref[...][None], axis=2, keepdims=True)
    s = r + pltpu.roll(r, N_MEM_HEADS, 1)
    m = jnp.max(s, axis=0, keepdims=True)
    p = jnp.exp(s - m)
    l = jnp.sum(p, axis=0, keepdims=True)
    o_ref[...] = jnp.sum((p / l) * mv_ref[...].reshape(tile), axis=0)


def _dec_cross(q8, mem_k, mem_v, layer):
    n = q8.shape[0]
    rows = mem_k.shape[2]
    qspec = pl.BlockSpec((None, SUBLANES, LANES), lambda i: (i, 0, 0))
    mspec = pl.BlockSpec((None, None, rows, LANES), lambda i: (layer, i, 0, 0))
    return pl.pallas_call(
        _dec_cross_kernel,
        grid=(n,),
        in_specs=[qspec, mspec, mspec],
        out_specs=qspec,
        out_shape=jax.ShapeDtypeStruct((n, SUBLANES, LANES), F32),
        compiler_params=_params(("arbitrary",)),
        name="dec_cross",
    )(q8, mem_k, mem_v)


def _dec_ffn_kernel(x1_ref, oc_ref, wco_ref, g_ref, wa_ref, wg_ref, cwa_ref, cwg_ref, wd_ref,
                    hist_ref, gf_ref, o_ref, nh_ref, *, final):
    x2 = x1_ref[...] + _dot(oc_ref[...].astype(BF16), wco_ref[...].astype(BF16))
    hb = _rms(x2, g_ref[...]).astype(BF16)
    acc = x2
    for c in range(N_FF_CHUNKS):
        ys = []
        for half, (w_ref, cw_ref) in enumerate(((wa_ref, cwa_ref), (wg_ref, cwg_ref))):
            lo = half * D_FF + c * FF_CHUNK
            u = _dot(hb, w_ref[c])
            h0 = hist_ref[:, lo:lo + FF_CHUNK]
            h1 = hist_ref[:, 2 * D_FF + lo:2 * D_FF + lo + FF_CHUNK]
            cw = cw_ref[c]
            ys.append(cw[0:1] * h0 + cw[1:2] * h1 + cw[2:3] * u)
            nh_ref[:, lo:lo + FF_CHUNK] = h1
            nh_ref[:, 2 * D_FF + lo:2 * D_FF + lo + FF_CHUNK] = u
        act = (jax.nn.silu(ys[1]) * ys[0]).astype(BF16)
        acc = acc + _dot(act, wd_ref[c])
    if final:
        acc = _rms(acc, gf_ref[...])
    o_ref[...] = acc


def _dec_ffn(x1, oc, w_co, norm3, wa, wg, cwa, cwg, wd, hist, gf2, layer, final):
    n, d = x1.shape
    nch = N_FF_CHUNKS
    full = lambda shape: pl.BlockSpec(shape, lambda i: (0,) * len(shape))
    return pl.pallas_call(
        functools.partial(_dec_ffn_kernel, final=final),
        grid=(1,),
        in_specs=[full((n, d)), full((n, d)),
                  pl.BlockSpec((None, d, d), lambda i: (layer, 0, 0)),
                  pl.BlockSpec((None, 1, d), lambda i: (layer, 0, 0)),
                  pl.BlockSpec((None, nch, d, FF_CHUNK), lambda i: (layer, 0, 0, 0)),
                  pl.BlockSpec((None, nch, d, FF_CHUNK), lambda i: (layer, 0, 0, 0)),
                  pl.BlockSpec((None, nch, 3, FF_CHUNK), lambda i: (layer, 0, 0, 0)),
                  pl.BlockSpec((None, nch, 3, FF_CHUNK), lambda i: (layer, 0, 0, 0)),
                  pl.BlockSpec((None, nch, FF_CHUNK, d), lambda i: (layer, 0, 0, 0)),
                  pl.BlockSpec((None, n, 4 * D_FF), lambda i: (layer, 0, 0)),
                  full((1, d))],
        out_specs=[full((n, d)), full((n, 4 * D_FF))],
        out_shape=[jax.ShapeDtypeStruct((n, d), F32), jax.ShapeDtypeStruct((n, 4 * D_FF), F32)],
        compiler_params=_params(("arbitrary",)),
        name="dec_ffn",
    )(x1, oc, w_co, norm3, wa, wg, cwa, cwg, wd, hist, gf2)


def kernel(x_prompt, x_sample, mem_prompt, cache_k, cache_v, state_conv, state_ffn_conv,
           cache_mem_k, cache_mem_v, page_table, w_in, w_out, conv_w, lambda_q1, lambda_k1,
           lambda_q2, lambda_k2, subln_g, norm_mix, norm_cross, norm_mem, w_cq, w_ck, w_cv,
           w_co, norm_ffn, w_up, ffn_conv_w, w_down, norm_final):
    depth = w_in.shape[0]
    bp, tp, d = x_prompt.shape
    bs, ts, _ = x_sample.shape
    n_pages = page_table.shape[1]
    page = cache_k.shape[2]
    past_len = n_pages * page
    n_phys = cache_k.shape[1]
    nch = N_FF_CHUNKS

    bf = lambda w: w.astype(BF16)
    chunked = lambda w: bf(w).reshape(depth, d, nch, FF_CHUNK).transpose(0, 2, 1, 3)
    wa_b = chunked(w_up[:, :, :D_FF])
    wg_b = chunked(w_up[:, :, D_FF:])
    wd_b = bf(w_down).reshape(depth, nch, FF_CHUNK, d)
    cw_chunked = lambda w: w.reshape(depth, 3, nch, FF_CHUNK).transpose(0, 2, 1, 3)
    cwa = cw_chunked(ffn_conv_w[:, :, :D_FF])
    cwg = cw_chunked(ffn_conv_w[:, :, D_FF:])
    row3 = lambda g: g.reshape(depth, 1, g.shape[-1])
    norm_mix3, norm_cross3, norm_mem3, norm_ffn3, subln3 = map(
        row3, (norm_mix, norm_cross, norm_mem, norm_ffn, subln_g))
    lams = tuple(map(row3, (lambda_q1, lambda_k1, lambda_q2, lambda_k2)))
    gf2 = norm_final.reshape(1, d)
    subln_col = subln_g.reshape(depth, V_HEAD_DIM, 1)

    tables_p = _rope_tables(jnp.arange(tp))
    tables_s = _rope_tables(past_len + jnp.arange(ts))

    cache_k4 = cache_k.transpose(0, 1, 3, 4, 5, 2).reshape(depth, n_phys, ATTN_WIDTH, page)
    cache_v4 = cache_v.reshape(depth, n_phys, N_DIFF_HEADS * page, V_HEAD_DIM)
    halves = MEM_HEAD_DIM // LANES
    mem_rows = lambda a: a.reshape(depth, bs, a.shape[2], N_MEM_HEADS, halves, LANES).transpose(
        0, 1, 2, 4, 3, 5).reshape(depth, bs, a.shape[2] * halves * N_MEM_HEADS, LANES)
    mem_k4 = mem_rows(cache_mem_k)
    mem_v4 = mem_rows(cache_mem_v)
    to_tile = lambda a: a.reshape(bs, N_MEM_HEADS, halves, LANES).transpose(0, 2, 1, 3).reshape(
        bs, halves * N_MEM_HEADS, LANES)
    from_tile = lambda a: a.reshape(bs, halves, N_MEM_HEADS, LANES).transpose(0, 2, 1, 3).reshape(bs, d)
    conv_hist = state_conv.reshape(depth, bs, 2 * CONV_WIDTH)
    ffn_hist = state_ffn_conv.reshape(depth, bs, 4 * D_FF)

    mk, mv, mkb, mvb = _memkv(mem_prompt, norm_mem3, w_ck, w_cv)

    xp = x_prompt
    xs = x_sample.reshape(bs * ts, d)
    cp_l, fp_l, ks_l, vs_l, cs_l, fs_l = ([] for _ in range(6))
    kt_all = jnp.zeros((depth, bp, ATTN_WIDTH, tp), F32)
    v_all = jnp.zeros((depth, bp, N_DIFF_HEADS * tp, V_HEAD_DIM), F32)
    for l in range(depth):
        lam_init = 0.8 - 0.6 * math.exp(-0.3 * l)
        final = l == depth - 1
        q1, q2, kt_all, v_all, kb, vt, co, chist = _mix_in(xp, norm_mix3, w_in, conv_w, tables_p, kt_all, v_all, l)
        at = _attn(q1, q2, kb, vt, lams, subln_col, l, lam_init)
        x2 = _mix_out(xp, at, co, w_out, norm_cross3, w_cq, mkb, mvb, w_co, l)
        xp, fhist = _ffn(x2, norm_ffn3, wa_b, wg_b, cwa, cwg, wd_b, gf2, l, final)
        cp_l.append(chist[:, SUBLANES - 2:, :])
        fp_l.append(fhist[:, :, :, SUBLANES - 2:, :].transpose(0, 3, 2, 1, 4).reshape(bp, 2, 2 * D_FF))
        qd, kd, vd, cod, nconv = _dec_in(xs, norm_mix3, w_in, conv_w, conv_hist, tables_s, l)
        atd = _dec_attn(page_table, qd, kd, vd, lams, subln3, cache_k4, cache_v4, l, lam_init)
        x1d, qc = _dec_mid(xs, atd, cod, w_out, norm_cross3, w_cq, l)
        oc = from_tile(_dec_cross(to_tile(qc), mem_k4, mem_v4, l))
        xs, nffn = _dec_ffn(x1d, oc, w_co, norm_ffn3, wa_b, wg_b, cwa, cwg, wd_b, ffn_hist, gf2, l, final)
        ks_l.append(kd)
        vs_l.append(vd)
        cs_l.append(nconv)
        fs_l.append(nffn)

    st = jnp.stack
    return (xp,
            xs.reshape(bs, ts, d),
            kt_all.reshape(depth, bp, N_DIFF_HEADS, 2, DIFF_HEAD_DIM, tp).transpose(0, 1, 5, 2, 3, 4),
            v_all.reshape(depth, bp, tp, N_DIFF_HEADS, V_HEAD_DIM),
            st(cp_l),
            st(fp_l),
            mk.reshape(depth, bp, -1, N_MEM_HEADS, MEM_HEAD_DIM),
            mv.reshape(depth, bp, -1, N_MEM_HEADS, MEM_HEAD_DIM),
            st(ks_l).reshape(depth, bs, ts, N_DIFF_HEADS, 2, DIFF_HEAD_DIM),
            st(vs_l).reshape(depth, bs, ts, N_DIFF_HEADS, V_HEAD_DIM),
            st(cs_l).reshape(depth, bs, 2, CONV_WIDTH),
            st(fs_l).reshape(depth, bs, 2, 2 * D_FF))
```

```python
import functools
import math

import jax
import jax.numpy as jnp
from jax import lax
from jax.experimental import pallas as pl
from jax.experimental.pallas import tpu as pltpu

F32 = jnp.float32
BF16 = jnp.bfloat16

D_MODEL = 1024
ATTN_WIDTH = 512
CONV_WIDTH = 512
N_DIFF_HEADS = 4
DIFF_HEAD_DIM = 64
V_HEAD_DIM = 2 * DIFF_HEAD_DIM
ROT_DIM = DIFF_HEAD_DIM // 4
ROPE_THETA = 500000.0
N_MEM_HEADS = 4
MEM_HEAD_DIM = D_MODEL // N_MEM_HEADS
D_FF = 11 * D_MODEL // 4
EPS = 1e-6
NEG_INF = -1e30
LOG2E = math.log2(math.e)

LANES = 128
SUBLANES = 8
VMEM_LIMIT_BYTES = 56 * 1024 * 1024

TM_MIX = 1024
MIX_SPLIT = 2
TQ = 512
TM_FFN = 1024
FF_CHUNK = 256
N_FF_CHUNKS = D_FF // FF_CHUNK
FF_GROUP = 4
PAGES_PER_CHUNK = 16
DEC_ATTN_SLOTS = 3
DEC_CROSS_BATCH = 4


def _dot(a, b):
    return jnp.dot(a, b, preferred_element_type=F32)


def _dot_nt(a, b):
    return lax.dot_general(a, b, (((1,), (1,)), ((), ())), preferred_element_type=F32)


def _rms(x, g):
    ms = jnp.mean(x * x, axis=-1, keepdims=True)
    return x * lax.rsqrt(ms + EPS) * g


def _rope(y, cos, sa, sb):
    outs = []
    for c in range(y.shape[1] // LANES):
        yc = y[:, c * LANES:(c + 1) * LANES]
        outs.append(yc * cos + pltpu.roll(yc, ROT_DIM // 2, 1) * sa
                    + pltpu.roll(yc, LANES - ROT_DIM // 2, 1) * sb)
    return jnp.concatenate(outs, axis=1)


def _rope_tables(pos):
    half = ROT_DIM // 2
    inv = jnp.power(ROPE_THETA, -jnp.arange(half, dtype=F32) * 2.0 / ROT_DIM)
    ang = pos.astype(F32)[:, None] * inv[None, :]
    cos, sin = jnp.cos(ang), jnp.sin(ang)
    n = pos.shape[0]
    ones = jnp.ones((n, DIFF_HEAD_DIM - ROT_DIM), F32)
    zeros = jnp.zeros_like(ones)
    z8 = jnp.zeros((n, half), F32)
    c64 = jnp.concatenate([cos, cos, ones], axis=1)
    sa64 = jnp.concatenate([z8, sin, zeros], axis=1)
    sb64 = jnp.concatenate([-sin, z8, zeros], axis=1)
    tile2 = lambda a: jnp.concatenate([a, a], axis=1)
    return tile2(c64), tile2(sa64), tile2(sb64)


def _lam(lq1, lk1, lq2, lk2, lam_init):
    return (jnp.exp(jnp.sum(lq1 * lk1, axis=-1, keepdims=True))
            - jnp.exp(jnp.sum(lq2 * lk2, axis=-1, keepdims=True)) + lam_init)


def _params(sem=None):
    return pltpu.CompilerParams(dimension_semantics=sem, vmem_limit_bytes=VMEM_LIMIT_BYTES)


def _cast_weights(first, pairs):
    @pl.when(first)
    def _():
        for src, dst in pairs:
            dst[...] = src[...].astype(BF16)


def _memkv_kernel(mem_ref, g_ref, wk_ref, wv_ref, mk_ref, mv_ref, mkb_ref, mvb_ref, wkb, wvb):
    _cast_weights(pl.program_id(1) == 0, ((wk_ref, wkb), (wv_ref, wvb)))
    hm = _rms(mem_ref[...], g_ref[...]).astype(BF16)
    mk = _dot(hm, wkb[...])
    mv = _dot(hm, wvb[...])
    mk_ref[...] = mk
    mv_ref[...] = mv
    mkb_ref[...] = mk.astype(BF16)
    mvb_ref[...] = mv.astype(BF16)


def _memkv(mem, g3, wk, wv):
    depth = wk.shape[0]
    b, n, d = mem.shape
    wspec = pl.BlockSpec((None, d, d), lambda l, i: (l, 0, 0))
    ospec = pl.BlockSpec((None, None, n, d), lambda l, i: (l, i, 0, 0))
    return pl.pallas_call(
        _memkv_kernel,
        grid=(depth, b),
        in_specs=[pl.BlockSpec((None, n, d), lambda l, i: (i, 0, 0)),
                  pl.BlockSpec((None, 1, d), lambda l, i: (l, 0, 0)),
                  wspec, wspec],
        out_specs=[ospec, ospec, ospec, ospec],
        out_shape=[jax.ShapeDtypeStruct((depth, b, n, d), F32)] * 2
        + [jax.ShapeDtypeStruct((depth, b, n, d), BF16)] * 2,
        scratch_shapes=[pltpu.VMEM((d, d), BF16)] * 2,
        compiler_params=_params(("arbitrary", "arbitrary")),
        name="memkv",
    )(mem, g3, wk, wv)


def _mix_in_kernel(x_ref, g_ref, w_ref, cw_ref, cos_ref, sa_ref, sb_ref, kt_all_ref, v_all_ref,
                   q1_ref, q2_ref, kt_ref, v_ref, kb_ref, vt_ref, co_ref, hist_ref,
                   cbuf, wb, *, tm):
    del kt_all_ref, v_all_ref
    t = pl.program_id(1)
    _cast_weights((pl.program_id(0) == 0) & (t == 0), ((w_ref, wb),))
    @pl.when(t == 0)
    def _():
        cbuf[0:SUBLANES, :] = jnp.zeros((SUBLANES, CONV_WIDTH), F32)

    cw = cw_ref[...]
    rb = tm // MIX_SPLIT
    lane = lax.broadcasted_iota(jnp.int32, (rb, ATTN_WIDTH), 1)
    first_map = (lane % V_HEAD_DIM) < DIFF_HEAD_DIM
    for r in range(MIX_SPLIT):
        rows = slice(r * rb, (r + 1) * rb)
        h = _rms(x_ref[rows, :], g_ref[...]).astype(BF16)
        cos, sa, sb = cos_ref[rows, :], sa_ref[rows, :], sb_ref[rows, :]

        def proj(i, h=h):
            return _dot(h, wb[:, i * ATTN_WIDTH:(i + 1) * ATTN_WIDTH])

        q = _rope(proj(0), cos, sa, sb) * (DIFF_HEAD_DIM ** -0.5 * LOG2E)
        q1_ref[rows, :] = jnp.where(first_map, q, 0.0).astype(BF16)
        q2_ref[rows, :] = jnp.where(first_map, 0.0, q).astype(BF16)
        k = _rope(proj(1), cos, sa, sb)
        kt_ref[:, rows] = k.T
        kb_ref[rows, :] = k.astype(BF16)
        v = proj(2)
        for hd in range(N_DIFF_HEADS):
            v_ref[pl.ds(r * rb * N_DIFF_HEADS + hd, rb, stride=N_DIFF_HEADS), :] = (
                v[:, hd * V_HEAD_DIM:(hd + 1) * V_HEAD_DIM])
        vt_ref[:, rows] = v.T.astype(BF16)

        gate_b = proj(3)
        cu = proj(4) * proj(5)
        lo = SUBLANES + r * rb
        cbuf[lo:lo + rb, :] = cu
        y = (cw[0:1] * cbuf[lo - 2:lo - 2 + rb, :]
             + cw[1:2] * cbuf[lo - 1:lo - 1 + rb, :]
             + cw[2:3] * cu)
        co_ref[rows, :] = (gate_b * y).astype(BF16)
    tail = cbuf[tm:tm + SUBLANES, :]
    hist_ref[...] = tail
    cbuf[0:SUBLANES, :] = tail


def _mix_in(x, norm3, w_in, conv_w, tables, kt_all, v_all, layer):
    b, t, d = x.shape
    tm = TM_MIX
    cos, sa, sb = tables
    tok = lambda w, dt: jax.ShapeDtypeStruct((b, t, w), dt)
    tspec = pl.BlockSpec((tm, LANES), lambda i, j: (j, 0))
    ospec = pl.BlockSpec((None, tm, ATTN_WIDTH), lambda i, j: (i, j, 0))
    tspec_t = pl.BlockSpec((None, ATTN_WIDTH, tm), lambda i, j: (i, 0, j))
    hbm = pl.BlockSpec(memory_space=pl.ANY)
    return pl.pallas_call(
        functools.partial(_mix_in_kernel, tm=tm),
        grid=(b, t // tm),
        in_specs=[pl.BlockSpec((None, tm, d), lambda i, j: (i, j, 0)),
                  pl.BlockSpec((None, 1, d), lambda i, j: (layer, 0, 0)),
                  pl.BlockSpec((None, d, w_in.shape[2]), lambda i, j: (layer, 0, 0), pipeline_mode=pl.Buffered(1)),
                  pl.BlockSpec((None, 3, CONV_WIDTH), lambda i, j: (layer, 0, 0)),
                  tspec, tspec, tspec, hbm, hbm],
        out_specs=[ospec, ospec,
                   pl.BlockSpec((None, None, ATTN_WIDTH, tm), lambda i, j: (layer, i, 0, j)),
                   pl.BlockSpec((None, None, N_DIFF_HEADS * tm, V_HEAD_DIM), lambda i, j: (layer, i, j, 0)),
                   ospec, tspec_t, ospec,
                   pl.BlockSpec((None, SUBLANES, CONV_WIDTH), lambda i, j: (i, 0, 0))],
        out_shape=[tok(ATTN_WIDTH, BF16), tok(ATTN_WIDTH, BF16),
                   jax.ShapeDtypeStruct(kt_all.shape, F32),
                   jax.ShapeDtypeStruct(v_all.shape, F32),
                   tok(ATTN_WIDTH, BF16), jax.ShapeDtypeStruct((b, ATTN_WIDTH, t), BF16), tok(CONV_WIDTH, BF16),
                   jax.ShapeDtypeStruct((b, SUBLANES, CONV_WIDTH), F32)],
        input_output_aliases={7: 2, 8: 3},
        scratch_shapes=[pltpu.VMEM((tm + SUBLANES, CONV_WIDTH), F32),
                        pltpu.VMEM((d, w_in.shape[2]), BF16)],
        compiler_params=_params(("arbitrary", "arbitrary")),
        name="mix_in",
    )(x, norm3, w_in, conv_w, cos, sa, sb, kt_all, v_all)


def _attn_kernel(qi_ref, kj_ref, q1_ref, q2_ref, k_ref, vt_ref, lq1_ref, lk1_ref, lq2_ref, lk2_ref, sg_ref,
                 o_ref, s_sc, m_sc, l_sc, acc_sc, *, tq, lam_init):
    i = qi_ref[pl.program_id(1)]
    j = kj_ref[pl.program_id(1)]
    ones_rows = 16

    @pl.when(j == 0)
    def _():
        m_sc[...] = jnp.full(m_sc.shape, NEG_INF, F32)
        l_sc[...] = jnp.zeros(l_sc.shape, F32)
        acc_sc[...] = jnp.zeros(acc_sc.shape, F32)

    def step(masked):
        for h in range(N_DIFF_HEADS):
            hs = slice(h * V_HEAD_DIM, (h + 1) * V_HEAD_DIM)
            kh = k_ref[:, hs]
            for c, q_ref in enumerate((q1_ref, q2_ref)):
                s_sc[2 * h + c] = _dot_nt(kh, q_ref[:, hs])
        s = s_sc[...]
        if masked:
            key = lax.broadcasted_iota(jnp.int32, (tq, tq), 0)
            qry = lax.broadcasted_iota(jnp.int32, (tq, tq), 1)
            s = jnp.where((key <= qry)[None], s, NEG_INF)
        m_old = m_sc[...]
        m_new = jnp.maximum(m_old, jnp.max(s, axis=1, keepdims=True))
        alpha = jnp.exp2(m_old - m_new)
        p = jnp.exp2(s - m_new).astype(BF16)
        m_sc[...] = m_new
        ones = jnp.ones((ones_rows, tq), BF16)
        for h in range(N_DIFF_HEADS):
            lhs = jnp.concatenate([vt_ref[h * V_HEAD_DIM:(h + 1) * V_HEAD_DIM, :], ones], axis=0)
            for c in range(2):
                idx = 2 * h + c
                r = _dot(lhs, p[idx])
                acc_sc[idx] = alpha[idx] * acc_sc[idx] + r[0:V_HEAD_DIM]
                l_sc[idx] = alpha[idx] * l_sc[idx] + r[V_HEAD_DIM:V_HEAD_DIM + 1]

    @pl.when(j < i)
    def _():
        step(False)

    @pl.when(j == i)
    def _():
        step(True)
        lam = _lam(lq1_ref[...], lk1_ref[...], lq2_ref[...], lk2_ref[...], lam_init)
        sg = sg_ref[...]
        for h in range(N_DIFF_HEADS):
            o = acc_sc[2 * h] / l_sc[2 * h] - lam * (acc_sc[2 * h + 1] / l_sc[2 * h + 1])
            ms = jnp.mean(o * o, axis=0, keepdims=True)
            o = o * lax.rsqrt(ms + EPS) * sg * (1.0 - lam_init)
            o_ref[:, h * V_HEAD_DIM:(h + 1) * V_HEAD_DIM] = o.T.astype(BF16)


def _attn(q1, q2, kb, vt, lams, sg_col, layer, lam_init):
    b, t, w = q1.shape
    tq = TQ
    nq = t // tq
    n_combo = 2 * N_DIFF_HEADS
    pairs = [(i, j) for i in range(nq) for j in range(i + 1)]
    qi = jnp.asarray([p[0] for p in pairs], jnp.int32)
    kj = jnp.asarray([p[1] for p in pairs], jnp.int32)
    qspec = pl.BlockSpec((None, tq, w), lambda bi, s, qi, kj: (bi, qi[s], 0))
    kspec = pl.BlockSpec((None, tq, w), lambda bi, s, qi, kj: (bi, kj[s], 0))
    vspec = pl.BlockSpec((None, w, tq), lambda bi, s, qi, kj: (bi, 0, kj[s]))
    lspec = pl.BlockSpec((None, 1, DIFF_HEAD_DIM), lambda bi, s, qi, kj: (layer, 0, 0))
    return pl.pallas_call(
        functools.partial(_attn_kernel, tq=tq, lam_init=lam_init),
        grid_spec=pltpu.PrefetchScalarGridSpec(
            num_scalar_prefetch=2,
            grid=(b, len(pairs)),
            in_specs=[qspec, qspec, kspec, vspec, lspec, lspec, lspec, lspec,
                      pl.BlockSpec((None, V_HEAD_DIM, 1), lambda bi, s, qi, kj: (layer, 0, 0))],
            out_specs=pl.BlockSpec((None, tq, w), lambda bi, s, qi, kj: (bi, qi[s], 0)),
            scratch_shapes=[pltpu.VMEM((n_combo, tq, tq), F32),
                            pltpu.VMEM((n_combo, 1, tq), F32),
                            pltpu.VMEM((n_combo, 1, tq), F32),
                            pltpu.VMEM((n_combo, V_HEAD_DIM, tq), F32)]),
        out_shape=jax.ShapeDtypeStruct((b, t, w), BF16),
        compiler_params=_params(("arbitrary", "arbitrary")),
        name="diff_attn",
    )(qi, kj, q1, q2, kb, vt, *lams, sg_col)


def _cross_attend(qc, mk_ref, mv_ref):
    outs = []
    for h in range(N_MEM_HEADS):
        hs = slice(h * MEM_HEAD_DIM, (h + 1) * MEM_HEAD_DIM)
        s = _dot_nt(qc[:, hs], mk_ref[:, hs])
        m = jnp.max(s, axis=1, keepdims=True)
        p = jnp.exp(s - m)
        l = jnp.sum(p, axis=1, keepdims=True)
        outs.append((_dot(p.astype(BF16), mv_ref[:, hs]) / l).astype(BF16))
    return jnp.concatenate(outs, axis=1)


def _mix_out_kernel(x_ref, at_ref, co_ref, wo_ref, g_ref, wq_ref, mk_ref, mv_ref, wco_ref, o_ref,
                    wob, wqb, wcob):
    _cast_weights((pl.program_id(0) == 0) & (pl.program_id(1) == 0),
                  ((wo_ref, wob), (wq_ref, wqb), (wco_ref, wcob)))
    x1 = x_ref[...] + _dot(jnp.concatenate([at_ref[...], co_ref[...]], axis=1), wob[...])
    hq = _rms(x1, g_ref[...]).astype(BF16)
    qc = (_dot(hq, wqb[...]) * (MEM_HEAD_DIM ** -0.5)).astype(BF16)
    o = _cross_attend(qc, mk_ref, mv_ref)
    o_ref[...] = x1 + _dot(o, wcob[...])


def _mix_out(x, at, co, w_out, norm3, w_cq, mkb, mvb, w_co, layer):
    b, t, d = x.shape
    tm = TM_MIX
    n_mem = mkb.shape[2]
    wspec = pl.BlockSpec((None, d, d), lambda i, j: (layer, 0, 0), pipeline_mode=pl.Buffered(1))
    hspec = pl.BlockSpec((None, tm, ATTN_WIDTH), lambda i, j: (i, j, 0))
    mspec = pl.BlockSpec((None, None, n_mem, d), lambda i, j: (layer, i, 0, 0))
    xspec = pl.BlockSpec((None, tm, d), lambda i, j: (i, j, 0))
    return pl.pallas_call(
        _mix_out_kernel,
        grid=(b, t // tm),
        in_specs=[xspec, hspec, hspec, wspec,
                  pl.BlockSpec((None, 1, d), lambda i, j: (layer, 0, 0)),
                  wspec, mspec, mspec, wspec],
        out_specs=xspec,
        out_shape=jax.ShapeDtypeStruct((b, t, d), F32),
        scratch_shapes=[pltpu.VMEM((d, d), BF16)] * 3,
        compiler_params=_params(("arbitrary", "arbitrary")),
        name="mix_out",
    )(x, at, co, w_out, norm3, w_cq, mkb, mvb, w_co)


def _ffn_kernel(x_ref, g_ref, wa_ref, wg_ref, cwa_ref, cwg_ref, wd_ref, gf_ref,
                o_ref, hist_ref, hb_sc, acc_sc, buf, car_a, car_g, *, tm, final):
    t = pl.program_id(1)
    x = x_ref[...]
    hb_sc[...] = _rms(x, g_ref[...]).astype(BF16)
    acc_sc[...] = x

    @pl.when(t == 0)
    def _():
        car_a[...] = jnp.zeros(car_a.shape, F32)
        car_g[...] = jnp.zeros(car_g.shape, F32)

    def conv(u, stage, car, cw, c):
        stage[0:SUBLANES, :] = car[c]
        stage[SUBLANES:SUBLANES + tm, :] = u
        y = (cw[0:1] * stage[SUBLANES - 2:SUBLANES - 2 + tm, :]
             + cw[1:2] * stage[SUBLANES - 1:SUBLANES - 1 + tm, :]
             + cw[2:3] * u)
        car[c] = u[tm - SUBLANES:, :]
        return y

    def gated(c, slot):
        hb = hb_sc[...]
        a = conv(_dot(hb, wa_ref[c]), buf.at[2 * slot], car_a, cwa_ref[c], c)
        g = conv(_dot(hb, wg_ref[c]), buf.at[2 * slot + 1], car_g, cwg_ref[c], c)
        return (jax.nn.silu(g) * a).astype(BF16)

    def group(first, size):
        act = jnp.concatenate([gated(first + s, s) for s in range(size)], axis=1)
        acc_sc[...] += _dot(act, wd_ref[pl.ds(first, size)].reshape(size * FF_CHUNK, wd_ref.shape[2]))

    def trip(k, carry):
        group(FF_GROUP * k, FF_GROUP)
        return carry

    lax.fori_loop(0, N_FF_CHUNKS // FF_GROUP, trip, 0)
    if N_FF_CHUNKS % FF_GROUP:
        group(N_FF_CHUNKS - N_FF_CHUNKS % FF_GROUP, N_FF_CHUNKS % FF_GROUP)
    hist_ref[:, 0] = car_a[...]
    hist_ref[:, 1] = car_g[...]
    out = acc_sc[...]
    if final:
        out = _rms(out, gf_ref[...])
    o_ref[...] = out


def _ffn(x, norm3, wa, wg, cwa, cwg, wd, gf2, layer, final):
    b, t, d = x.shape
    tm = TM_FFN
    nch = N_FF_CHUNKS
    xspec = pl.BlockSpec((None, tm, d), lambda i, j: (i, j, 0))
    single = pl.Buffered(1)
    return pl.pallas_call(
        functools.partial(_ffn_kernel, tm=tm, final=final),
        grid=(b, t // tm),
        in_specs=[xspec,
                  pl.BlockSpec((None, 1, d), lambda i, j: (layer, 0, 0)),
                  pl.BlockSpec((None, nch, d, FF_CHUNK), lambda i, j: (layer, 0, 0, 0), pipeline_mode=single),
                  pl.BlockSpec((None, nch, d, FF_CHUNK), lambda i, j: (layer, 0, 0, 0), pipeline_mode=single),
                  pl.BlockSpec((None, nch, 3, FF_CHUNK), lambda i, j: (layer, 0, 0, 0)),
                  pl.BlockSpec((None, nch, 3, FF_CHUNK), lambda i, j: (layer, 0, 0, 0)),
                  pl.BlockSpec((None, nch, FF_CHUNK, d), lambda i, j: (layer, 0, 0, 0), pipeline_mode=single),
                  pl.BlockSpec((1, d), lambda i, j: (0, 0))],
        out_specs=[xspec,
                   pl.BlockSpec((None, nch, 2, SUBLANES, FF_CHUNK), lambda i, j: (i, 0, 0, 0, 0))],
        out_shape=[jax.ShapeDtypeStruct((b, t, d), F32),
                   jax.ShapeDtypeStruct((b, nch, 2, SUBLANES, FF_CHUNK), F32)],
        scratch_shapes=[pltpu.VMEM((tm, d), BF16),
                        pltpu.VMEM((tm, d), F32),
                        pltpu.VMEM((2 * FF_GROUP, tm + SUBLANES, FF_CHUNK), F32),
                        pltpu.VMEM((nch, SUBLANES, FF_CHUNK), F32),
                        pltpu.VMEM((nch, SUBLANES, FF_CHUNK), F32)],
        compiler_params=_params(("arbitrary", "arbitrary")),
        name="ffn",
    )(x, norm3, wa, wg, cwa, cwg, wd, gf2)


def _dec_in_kernel(x_ref, g_ref, w_ref, cw_ref, hist_ref, cos_ref, sa_ref, sb_ref,
                   q_ref, k_ref, v_ref, co_ref, nh_ref):
    h = _rms(x_ref[...], g_ref[...]).astype(BF16)
    cos, sa, sb = cos_ref[...], sa_ref[...], sb_ref[...]

    def proj(i):
        return _dot(h, w_ref[:, i * ATTN_WIDTH:(i + 1) * ATTN_WIDTH].astype(BF16))

    q_ref[...] = _rope(proj(0), cos, sa, sb) * (DIFF_HEAD_DIM ** -0.5)
    k_ref[...] = _rope(proj(1), cos, sa, sb)
    v_ref[...] = proj(2)
    gate_b = proj(3)
    cu = proj(4) * proj(5)
    h0 = hist_ref[:, 0:CONV_WIDTH]
    h1 = hist_ref[:, CONV_WIDTH:2 * CONV_WIDTH]
    cw = cw_ref[...]
    y = cw[0:1] * h0 + cw[1:2] * h1 + cw[2:3] * cu
    co_ref[...] = (gate_b * y).astype(BF16)
    nh_ref[:, 0:CONV_WIDTH] = h1
    nh_ref[:, CONV_WIDTH:2 * CONV_WIDTH] = cu


def _dec_in(x, norm3, w_in, conv_w, hist, tables, layer):
    n, d = x.shape
    cos, sa, sb = tables
    full = lambda shape: pl.BlockSpec(shape, lambda i: (0,) * len(shape))
    ospec = full((n, ATTN_WIDTH))
    return pl.pallas_call(
        _dec_in_kernel,
        grid=(1,),
        in_specs=[full((n, d)),
                  pl.BlockSpec((None, 1, d), lambda i: (layer, 0, 0)),
                  pl.BlockSpec((None, d, w_in.shape[2]), lambda i: (layer, 0, 0)),
                  pl.BlockSpec((None, 3, CONV_WIDTH), lambda i: (layer, 0, 0)),
                  pl.BlockSpec((None, n, 2 * CONV_WIDTH), lambda i: (layer, 0, 0)),
                  full((1, LANES)), full((1, LANES)), full((1, LANES))],
        out_specs=[ospec, ospec, ospec, ospec, full((n, 2 * CONV_WIDTH))],
        out_shape=[jax.ShapeDtypeStruct((n, ATTN_WIDTH), F32)] * 3
        + [jax.ShapeDtypeStruct((n, CONV_WIDTH), BF16),
           jax.ShapeDtypeStruct((n, 2 * CONV_WIDTH), F32)],
        compiler_params=_params(("arbitrary",)),
        name="dec_in",
    )(x, norm3, w_in, conv_w, hist, cos, sa, sb)


def _group_select(n_rows, width, group):
    r = lax.broadcasted_iota(jnp.int32, (n_rows, width), 0)
    c = lax.broadcasted_iota(jnp.int32, (n_rows, width), 1)
    return r == c // group


def _dec_attn_kernel(pt_ref, q_ref, kn_ref, vn_ref, lq1_ref, lk1_ref, lq2_ref, lk2_ref, sg_ref,
                     kt_hbm, v_hbm, o_ref, buf, sem, s_sc, p_sc, x_sc,
                     *, layer, lam_init, n_req, n_pages, page):
    cp = PAGES_PER_CHUNK
    nch = n_pages // cp
    width = ATTN_WIDTH
    n_combo = 2 * N_DIFF_HEADS
    jobs = 2 * nch

    def copies(b, job, slot):
        src = kt_hbm if job < nch else v_hbm
        c = job % nch
        return [pltpu.make_async_copy(src.at[layer, pt_ref[b, c * cp + g]], buf.at[slot, g], sem.at[slot])
                for g in range(cp)]

    def start(b, job, slot):
        for cpy in copies(b, job, slot):
            cpy.start()

    def wait(b, job, slot):
        for cpy in copies(b, job, slot):
            cpy.wait()

    lam = _lam(lq1_ref[...], lk1_ref[...], lq2_ref[...], lk2_ref[...], lam_init)
    sg = sg_ref[...]
    combo = lax.broadcasted_iota(jnp.int32, (n_combo, 1), 0)
    feat_sel = _group_select(n_combo, width, DIFF_HEAD_DIM)
    r = lax.broadcasted_iota(jnp.int32, (page, N_DIFF_HEADS * page), 0)
    c = lax.broadcasted_iota(jnp.int32, (page, N_DIFF_HEADS * page), 1)
    dilate = jnp.where(c // N_DIFF_HEADS == r, 1.0, 0.0).astype(BF16)
    rk = lax.broadcasted_iota(jnp.int32, (cp * n_combo, N_DIFF_HEADS * page), 0) % n_combo
    ch = lax.broadcasted_iota(jnp.int32, (cp * n_combo, N_DIFF_HEADS * page), 1) % N_DIFF_HEADS
    head_sel = rk // 2 == ch

    ahead = DEC_ATTN_SLOTS - 1

    def slot_of(b, job):
        return lax.rem(b * jobs + job, DEC_ATTN_SLOTS)

    for first in range(ahead):
        start(0, first, first)

    def request(b, carry):
        qrow = q_ref[b]
        qcol = jnp.broadcast_to(qrow, (page, width)).T
        vn = vn_ref[b]
        s_self = jnp.sum(jnp.where(feat_sel, jnp.broadcast_to(qrow * kn_ref[b], (n_combo, width)), 0.0),
                         axis=1, keepdims=True)
        o8 = None
        for job in range(jobs):
            slot = slot_of(b, job)
            nxt = job + ahead
            if nxt < jobs:
                start(b, nxt, slot_of(b, nxt))
            else:
                @pl.when(b + 1 < n_req)
                def _(nxt=nxt):
                    start(b + 1, nxt - jobs, slot_of(b + 1, nxt - jobs))
            wait(b, job, slot)
            if job < nch:
                def k_page(g, carry2, job=job, slot=slot):
                    prod = buf[slot, g] * qcol
                    s_sc[job * cp + g] = jnp.sum(prod.reshape(n_combo, DIFF_HEAD_DIM, page), axis=1)
                    return carry2

                lax.fori_loop(0, cp, k_page, 0)
                if job == nch - 1:
                    s = s_sc[...]
                    m = jnp.maximum(jnp.max(jnp.max(s, axis=0), axis=1, keepdims=True), s_self)
                    p = jnp.exp(s - m)
                    p_self = jnp.exp(s_self - m)
                    l = jnp.sum(jnp.sum(p, axis=0), axis=1, keepdims=True) + p_self
                    scale = jnp.where(combo % 2 == 0, 1.0, -lam) / l
                    p_sc[...] = p * scale
                    vn8 = jnp.concatenate(
                        [vn[:, (k // 2) * V_HEAD_DIM:(k // 2 + 1) * V_HEAD_DIM] for k in range(n_combo)], axis=0)
                    o8 = (p_self * scale) * vn8
            else:
                jv = job - nch
                pch = p_sc[jv * cp:(jv + 1) * cp].reshape(cp * n_combo, page).astype(BF16)
                x_sc[...] = jnp.where(head_sel, _dot(pch, dilate), 0.0)

                def v_page(g, acc, slot=slot):
                    xg = x_sc[pl.ds(pl.multiple_of(g * n_combo, n_combo), n_combo), :].astype(BF16)
                    return acc + _dot(xg, buf[slot, g].astype(BF16))

                o8 = lax.fori_loop(0, cp, v_page, o8, unroll=4)

        o_ref[b] = jnp.concatenate(
            [_rms(o8[2 * h:2 * h + 1] + o8[2 * h + 1:2 * h + 2], sg) * (1.0 - lam_init)
             for h in range(N_DIFF_HEADS)], axis=1)
        return carry

    lax.fori_loop(0, n_req, request, 0)


def _dec_attn(page_table, q, kn, vn, lams, sg3, cache_kt, cache_v2, layer, lam_init):
    n_req, n_pages = page_table.shape
    page = cache_kt.shape[3]
    cp = PAGES_PER_CHUNK
    n_combo = 2 * N_DIFF_HEADS
    vmem = pl.BlockSpec(memory_space=pltpu.VMEM)
    smem = pl.BlockSpec(memory_space=pltpu.SMEM)
    hbm = pl.BlockSpec(memory_space=pl.ANY)
    return pl.pallas_call(
        functools.partial(_dec_attn_kernel, layer=layer, lam_init=lam_init, n_req=n_req,
                          n_pages=n_pages, page=page),
        in_specs=[smem, vmem, vmem, vmem, vmem, vmem, vmem, vmem, vmem, hbm, hbm],
        out_specs=vmem,
        out_shape=jax.ShapeDtypeStruct((n_req, 1, ATTN_WIDTH), F32),
        scratch_shapes=[pltpu.VMEM((DEC_ATTN_SLOTS, cp, ATTN_WIDTH, page), F32),
                        pltpu.SemaphoreType.DMA((DEC_ATTN_SLOTS,)),
                        pltpu.VMEM((n_pages, n_combo, page), F32),
                        pltpu.VMEM((n_pages, n_combo, page), F32),
                        pltpu.VMEM((cp * n_combo, N_DIFF_HEADS * page), F32)],
        compiler_params=pltpu.CompilerParams(vmem_limit_bytes=VMEM_LIMIT_BYTES),
        name="dec_attn",
    )(page_table, *[a.reshape(n_req, 1, ATTN_WIDTH) for a in (q, kn, vn)],
      *[a[layer] for a in lams], sg3[layer], cache_kt, cache_v2).reshape(n_req, ATTN_WIDTH)


def _dec_mid_kernel(x_ref, at_ref, co_ref, wo_ref, g_ref, wq_ref, x1_ref, qc_ref):
    x1 = (x_ref[...] + _dot(at_ref[...].astype(BF16), wo_ref[0:ATTN_WIDTH, :].astype(BF16))
          + _dot(co_ref[...], wo_ref[ATTN_WIDTH:ATTN_WIDTH + CONV_WIDTH, :].astype(BF16)))
    x1_ref[...] = x1
    hq = _rms(x1, g_ref[...]).astype(BF16)
    qc_ref[...] = _dot(hq, wq_ref[...].astype(BF16)) * (MEM_HEAD_DIM ** -0.5)


def _dec_mid(x, at, co, w_out, norm3, w_cq, layer):
    n, d = x.shape
    full = lambda shape: pl.BlockSpec(shape, lambda i: (0,) * len(shape))
    wspec = pl.BlockSpec((None, d, d), lambda i: (layer, 0, 0))
    return pl.pallas_call(
        _dec_mid_kernel,
        grid=(1,),
        in_specs=[full((n, d)), full((n, ATTN_WIDTH)), full((n, CONV_WIDTH)), wspec,
                  pl.BlockSpec((None, 1, d), lambda i: (layer, 0, 0)), wspec],
        out_specs=[full((n, d)), full((n, d))],
        out_shape=[jax.ShapeDtypeStruct((n, d), F32)] * 2,
        compiler_params=_params(("arbitrary",)),
        name="dec_mid",
    )(x, at, co, w_out, norm3, w_cq)


def _dec_cross_kernel(q_ref, mk_ref, mv_ref, o_ref):
    tile = (mk_ref.shape[1] // SUBLANES, SUBLANES, LANES)
    for g in range(q_ref.shape[0]):
        r = jnp.sum(mk_ref[g].reshape(tile) * q_ref[g][None], axis=2, keepdims=True)
        s = r + pltpu.roll(r, N_MEM_HEADS, 1)
        m = jnp.max(s, axis=0, keepdims=True)
        p = jnp.exp(s - m)
        l = jnp.sum(p, axis=0, keepdims=True)
        o_ref[g] = jnp.sum((p / l) * mv_ref[g].reshape(tile), axis=0)


def _dec_cross(q8, mem_k, mem_v, layer):
    n = q8.shape[0]
    rows = mem_k.shape[2]
    gb = DEC_CROSS_BATCH
    qspec = pl.BlockSpec((gb, SUBLANES, LANES), lambda i: (i, 0, 0))
    mspec = pl.BlockSpec((None, gb, rows, LANES), lambda i: (layer, i, 0, 0))
    return pl.pallas_call(
        _dec_cross_kernel,
        grid=(n // gb,),
        in_specs=[qspec, mspec, mspec],
        out_specs=qspec,
        out_shape=jax.ShapeDtypeStruct((n, SUBLANES, LANES), F32),
        compiler_params=_params(("arbitrary",)),
        name="dec_cross",
    )(q8, mem_k, mem_v)


def _dec_ffn_kernel(x1_ref, oc_ref, wco_ref, g_ref, wa_ref, wg_ref, cwa_ref, cwg_ref, wd_ref,
                    hist_ref, gf_ref, o_ref, nh_ref, *, final):
    x2 = x1_ref[...] + _dot(oc_ref[...].astype(BF16), wco_ref[...].astype(BF16))
    hb = _rms(x2, g_ref[...]).astype(BF16)
    acc = x2
    for c in range(N_FF_CHUNKS):
        ys = []
        for half, (w_ref, cw_ref) in enumerate(((wa_ref, cwa_ref), (wg_ref, cwg_ref))):
            lo = half * D_FF + c * FF_CHUNK
            u = _dot(hb, w_ref[c])
            h0 = hist_ref[:, lo:lo + FF_CHUNK]
            h1 = hist_ref[:, 2 * D_FF + lo:2 * D_FF + lo + FF_CHUNK]
            cw = cw_ref[c]
            ys.append(cw[0:1] * h0 + cw[1:2] * h1 + cw[2:3] * u)
            nh_ref[:, lo:lo + FF_CHUNK] = h1
            nh_ref[:, 2 * D_FF + lo:2 * D_FF + lo + FF_CHUNK] = u
        act = (jax.nn.silu(ys[1]) * ys[0]).astype(BF16)
        acc = acc + _dot(act, wd_ref[c])
    if final:
        acc = _rms(acc, gf_ref[...])
    o_ref[...] = acc


def _dec_ffn(x1, oc, w_co, norm3, wa, wg, cwa, cwg, wd, hist, gf2, layer, final):
    n, d = x1.shape
    nch = N_FF_CHUNKS
    full = lambda shape: pl.BlockSpec(shape, lambda i: (0,) * len(shape))
    return pl.pallas_call(
        functools.partial(_dec_ffn_kernel, final=final),
        grid=(1,),
        in_specs=[full((n, d)), full((n, d)),
                  pl.BlockSpec((None, d, d), lambda i: (layer, 0, 0)),
                  pl.BlockSpec((None, 1, d), lambda i: (layer, 0, 0)),
                  pl.BlockSpec((None, nch, d, FF_CHUNK), lambda i: (layer, 0, 0, 0)),
                  pl.BlockSpec((None, nch, d, FF_CHUNK), lambda i: (layer, 0, 0, 0)),
                  pl.BlockSpec((None, nch, 3, FF_CHUNK), lambda i: (layer, 0, 0, 0)),
                  pl.BlockSpec((None, nch, 3, FF_CHUNK), lambda i: (layer, 0, 0, 0)),
                  pl.BlockSpec((None, nch, FF_CHUNK, d), lambda i: (layer, 0, 0, 0)),
                  pl.BlockSpec((None, n, 4 * D_FF), lambda i: (layer, 0, 0)),
                  full((1, d))],
        out_specs=[full((n, d)), full((n, 4 * D_FF))],
        out_shape=[jax.ShapeDtypeStruct((n, d), F32), jax.ShapeDtypeStruct((n, 4 * D_FF), F32)],
        compiler_params=_params(("arbitrary",)),
        name="dec_ffn",
    )(x1, oc, w_co, norm3, wa, wg, cwa, cwg, wd, hist, gf2)


def kernel(x_prompt, x_sample, mem_prompt, cache_k, cache_v, state_conv, state_ffn_conv,
           cache_mem_k, cache_mem_v, page_table, w_in, w_out, conv_w, lambda_q1, lambda_k1,
           lambda_q2, lambda_k2, subln_g, norm_mix, norm_cross, norm_mem, w_cq, w_ck, w_cv,
           w_co, norm_ffn, w_up, ffn_conv_w, w_down, norm_final):
    depth = w_in.shape[0]
    bp, tp, d = x_prompt.shape
    bs, ts, _ = x_sample.shape
    n_pages = page_table.shape[1]
    page = cache_k.shape[2]
    past_len = n_pages * page
    n_phys = cache_k.shape[1]
    nch = N_FF_CHUNKS

    bf = lambda w: w.astype(BF16)
    chunked = lambda w: bf(w).reshape(depth, d, nch, FF_CHUNK).transpose(0, 2, 1, 3)
    wa_b = chunked(w_up[:, :, :D_FF])
    wg_b = chunked(w_up[:, :, D_FF:])
    wd_b = bf(w_down).reshape(depth, nch, FF_CHUNK, d)
    cw_chunked = lambda w: w.reshape(depth, 3, nch, FF_CHUNK).transpose(0, 2, 1, 3)
    cwa = cw_chunked(ffn_conv_w[:, :, :D_FF])
    cwg = cw_chunked(ffn_conv_w[:, :, D_FF:])
    row3 = lambda g: g.reshape(depth, 1, g.shape[-1])
    norm_mix3, norm_cross3, norm_mem3, norm_ffn3, subln3 = map(
        row3, (norm_mix, norm_cross, norm_mem, norm_ffn, subln_g))
    lams = tuple(map(row3, (lambda_q1, lambda_k1, lambda_q2, lambda_k2)))
    gf2 = norm_final.reshape(1, d)
    subln_col = subln_g.reshape(depth, V_HEAD_DIM, 1)

    tables_p = _rope_tables(jnp.arange(tp))
    tables_s = _rope_tables(past_len + jnp.arange(ts))

    cache_k4 = cache_k.transpose(0, 1, 3, 4, 5, 2).reshape(depth, n_phys, ATTN_WIDTH, page)
    cache_v4 = cache_v.reshape(depth, n_phys, N_DIFF_HEADS * page, V_HEAD_DIM)
    halves = MEM_HEAD_DIM // LANES
    mem_rows = lambda a: a.reshape(depth, bs, a.shape[2], N_MEM_HEADS, halves, LANES).transpose(
        0, 1, 2, 4, 3, 5).reshape(depth, bs, a.shape[2] * halves * N_MEM_HEADS, LANES)
    mem_k4 = mem_rows(cache_mem_k)
    mem_v4 = mem_rows(cache_mem_v)
    to_tile = lambda a: a.reshape(bs, N_MEM_HEADS, halves, LANES).transpose(0, 2, 1, 3).reshape(
        bs, halves * N_MEM_HEADS, LANES)
    from_tile = lambda a: a.reshape(bs, halves, N_MEM_HEADS, LANES).transpose(0, 2, 1, 3).reshape(bs, d)
    conv_hist = state_conv.reshape(depth, bs, 2 * CONV_WIDTH)
    ffn_hist = state_ffn_conv.reshape(depth, bs, 4 * D_FF)

    mk, mv, mkb, mvb = _memkv(mem_prompt, norm_mem3, w_ck, w_cv)

    xp = x_prompt
    xs = x_sample.reshape(bs * ts, d)
    cp_l, fp_l, ks_l, vs_l, cs_l, fs_l = ([] for _ in range(6))
    kt_all = jnp.zeros((depth, bp, ATTN_WIDTH, tp), F32)
    v_all = jnp.zeros((depth, bp, N_DIFF_HEADS * tp, V_HEAD_DIM), F32)
    for l in range(depth):
        lam_init = 0.8 - 0.6 * math.exp(-0.3 * l)
        final = l == depth - 1
        q1, q2, kt_all, v_all, kb, vt, co, chist = _mix_in(xp, norm_mix3, w_in, conv_w, tables_p, kt_all, v_all, l)
        at = _attn(q1, q2, kb, vt, lams, subln_col, l, lam_init)
        x2 = _mix_out(xp, at, co, w_out, norm_cross3, w_cq, mkb, mvb, w_co, l)
        xp, fhist = _ffn(x2, norm_ffn3, wa_b, wg_b, cwa, cwg, wd_b, gf2, l, final)
        cp_l.append(chist[:, SUBLANES - 2:, :])
        fp_l.append(fhist[:, :, :, SUBLANES - 2:, :].transpose(0, 3, 2, 1, 4).reshape(bp, 2, 2 * D_FF))
        qd, kd, vd, cod, nconv = _dec_in(xs, norm_mix3, w_in, conv_w, conv_hist, tables_s, l)
        atd = _dec_attn(page_table, qd, kd, vd, lams, subln3, cache_k4, cache_v4, l, lam_init)
        x1d, qc = _dec_mid(xs, atd, cod, w_out, norm_cross3, w_cq, l)
        oc = from_tile(_dec_cross(to_tile(qc), mem_k4, mem_v4, l))
        xs, nffn = _dec_ffn(x1d, oc, w_co, norm_ffn3, wa_b, wg_b, cwa, cwg, wd_b, ffn_hist, gf2, l, final)
        ks_l.append(kd)
        vs_l.append(vd)
        cs_l.append(nconv)
        fs_l.append(nffn)

    st = jnp.stack
    return (xp,
            xs.reshape(bs, ts, d),
            kt_all.reshape(depth, bp, N_DIFF_HEADS, 2, DIFF_HEAD_DIM, tp).transpose(0, 1, 5, 2, 3, 4),
            v_all.reshape(depth, bp, tp, N_DIFF_HEADS, V_HEAD_DIM),
            st(cp_l),
            st(fp_l),
            mk.reshape(depth, bp, -1, N_MEM_HEADS, MEM_HEAD_DIM),
            mv.reshape(depth, bp, -1, N_MEM_HEADS, MEM_HEAD_DIM),
            st(ks_l).reshape(depth, bs, ts, N_DIFF_HEADS, 2, DIFF_HEAD_DIM),
            st(vs_l).reshape(depth, bs, ts, N_DIFF_HEADS, V_HEAD_DIM),
            st(cs_l).reshape(depth, bs, 2, CONV_WIDTH),
            st(fs_l).reshape(depth, bs, 2, 2 * D_FF))
```
